```python
import math
import jax, jax.numpy as jnp
from jax import lax
import numpy as np

D_MODEL = 1024
BATCH = 8
SEQ = 2048
DEPTH = 2
DEC_BATCH = 128
DEC_SEQ = 1
PAST_LEN = 16384
PAGE_SIZE = 128

RW_HEADS = 8
RW_HEAD_DIM = 64
RW_WIDTH = RW_HEADS * RW_HEAD_DIM
RW_DECAY_RANK = 64
RW_ICLR_RANK = 64
RW_SHIFT_W = 3 * RW_WIDTH + RW_DECAY_RANK + RW_ICLR_RANK
RW_GN_EPS = 64e-5
MLA_HEADS = 8
MLA_NOPE = 64
MLA_ROPE = 32
MLA_V = 64
MLA_Q_RANK = 256
MLA_KV_RANK = 128
MLA_WIDTH = MLA_HEADS * MLA_V
MLA_ROW = MLA_KV_RANK + MLA_ROPE
ROPE_THETA = 10000.0
MLA_Q_BLOCK = 128
MB_HEADS = 8
MB_KV_HEADS = 2
MB_HEAD_DIM = 64
MB_GROUP = MB_HEADS // MB_KV_HEADS
MB_WIDTH = MB_HEADS * MB_HEAD_DIM
MB_BLOCK = 256
MB_TOPK = 3
MB_Q_BLOCK = 32
N_BRANCH = 3
BR_WIDTH = RW_WIDTH
IN_WIDTHS = (RW_SHIFT_W, MLA_Q_RANK, MLA_KV_RANK, MLA_ROPE, MB_WIDTH, MB_KV_HEADS * MB_HEAD_DIM,
             MB_KV_HEADS * MB_HEAD_DIM, N_BRANCH * BR_WIDTH, N_BRANCH * D_MODEL)
IN_W = (RW_SHIFT_W + MLA_Q_RANK + MLA_KV_RANK + MLA_ROPE + MB_WIDTH + 2 * MB_KV_HEADS * MB_HEAD_DIM
        + N_BRANCH * BR_WIDTH + N_BRANCH * D_MODEL)
DEEPNORM_ALPHA = (2 * DEPTH) ** 0.25
DEEPNORM_BETA = (8 * DEPTH) ** -0.25

kernel_name = 'hybrid_rwkv7_mla_moba_deepnorm_step'

F32 = jnp.float32


def _rmsnorm(x, g, eps=1e-6):
    xf = x.astype(F32)
    y = xf * lax.rsqrt(jnp.mean(xf * xf, -1, keepdims=True) + eps)
    return (y * g.astype(F32)).astype(x.dtype)


def _layernorm(x, g, b, eps=1e-5):
    xf = x.astype(F32)
    mu = jnp.mean(xf, -1, keepdims=True)
    var = jnp.mean(jnp.square(xf - mu), -1, keepdims=True)
    return ((xf - mu) * lax.rsqrt(var + eps) * g.astype(F32) + b.astype(F32)).astype(x.dtype)


def _rope(x, pos):
    half = MLA_ROPE // 2
    inv = ROPE_THETA ** (-jnp.arange(half, dtype=F32) / half)
    ang = pos.astype(F32)[:, None] * inv[None, :]
    shp = (pos.shape[0],) + (1,) * (x.ndim - 3) + (half,)
    cos = jnp.cos(ang).reshape(shp)
    sin = jnp.sin(ang).reshape(shp)
    xf = x.astype(F32)
    x1, x2 = xf[..., :half], xf[..., half:]
    return jnp.concatenate([x1 * cos - x2 * sin, x2 * cos + x1 * sin], -1).astype(x.dtype)


def _sweep(fn, blk, pos, *qs):
    t = pos.shape[0]
    if t <= blk or t % blk:
        return fn(pos, *qs)
    nb = t // blk

    def split(a):
        return jnp.moveaxis(a.reshape((a.shape[0], nb, blk) + a.shape[2:]), 1, 0)

    out = lax.map(lambda args: fn(*args), (pos.reshape(nb, blk),) + tuple(split(a) for a in qs))
    out = jnp.moveaxis(out, 0, 1)
    return out.reshape((out.shape[0], t) + out.shape[3:])


def _split_proj(p):
    pts, acc = [], 0
    for w in IN_WIDTHS[:-1]:
        acc += w
        pts.append(acc)
    return jnp.split(p, pts, axis=-1)


def _rwkv_branch(cols, shift_prev, wkv0, mu, w0, w2, a0, a2, k_k, k_a, r_k, gn_g, gn_b):
    b, t, _ = cols.shape
    prev = jnp.concatenate([shift_prev[:, None].astype(cols.dtype), cols[:, :-1]], axis=1)
    mixed = cols + (prev - cols) * mu
    r, k, v, wd, ad = jnp.split(mixed, [RW_WIDTH, 2 * RW_WIDTH, 3 * RW_WIDTH, 3 * RW_WIDTH + RW_DECAY_RANK], axis=-1)
    w_log = -jax.nn.softplus(-(w0 + jnp.tanh(wd) @ w2).astype(F32)) - 0.5
    decay = jnp.exp(-jnp.exp(w_log))
    a = jax.nn.sigmoid((a0 + ad @ a2).astype(F32))
    kf = k.astype(F32)
    heads = lambda z: z.reshape(b, t, RW_HEADS, RW_HEAD_DIM)
    kk = heads(kf * k_k.astype(F32))
    kk = kk * lax.rsqrt(jnp.maximum(jnp.sum(kk * kk, -1, keepdims=True), 1e-24))
    k_mod = heads(kf * (1.0 + (a - 1.0) * k_a.astype(F32)))
    r_h = heads(r.astype(F32))
    v_h = heads(v.astype(F32))
    xs = tuple(jnp.moveaxis(z, 1, 0) for z in (r_h, heads(decay), k_mod, v_h, kk, heads(a)))

    def step(s, inp):
        r_t, w_t, k_t, v_t, kk_t, a_t = inp
        sa = jnp.einsum('bhvk,bhk->bhv', s, -kk_t)
        s = s * w_t[:, :, None, :] + sa[..., None] * (kk_t * a_t)[:, :, None, :] + v_t[..., None] * k_t[:, :, None, :]
        return s, jnp.einsum('bhvk,bhk->bhv', s, r_t)

    s_fin, y = lax.scan(step, wkv0.astype(F32), xs)
    y = jnp.moveaxis(y, 0, 1)
    ym = jnp.mean(y, -1, keepdims=True)
    yv = jnp.mean(jnp.square(y - ym), -1, keepdims=True)
    yn = ((y - ym) * lax.rsqrt(yv + RW_GN_EPS)).reshape(b, t, RW_WIDTH) * gn_g.astype(F32) + gn_b.astype(F32)
    bonus = jnp.sum(r_h * k_mod * r_k.astype(F32), -1, keepdims=True) * v_h
    out = yn + bonus.reshape(b, t, RW_WIDTH)
    return out.astype(cols.dtype), s_fin, cols[:, -1]


def _mla_branch(qd, kvd, kr, pos, past_rows, q_norm, w_uq, kv_norm, w_uk, w_uv):
    b, t, _ = qd.shape
    q = (_rmsnorm(qd, q_norm) @ w_uq).reshape(b, t, MLA_HEADS, MLA_NOPE + MLA_ROPE)
    q_nope = q[..., :MLA_NOPE]
    q_rope = _rope(q[..., MLA_NOPE:], pos)
    c_kv = _rmsnorm(kvd, kv_norm)
    k_rope = _rope(kr, pos)
    new_rows = jnp.concatenate([c_kv, k_rope], -1)
    rows = jnp.concatenate([past_rows.astype(new_rows.dtype), new_rows], 1)
    c_all, kr_all = rows[..., :MLA_KV_RANK], rows[..., MLA_KV_RANK:]
    key_pos = jnp.arange(rows.shape[1])
    q_lat = jnp.einsum('bthn,chn->bthc', q_nope, w_uk)
    scale = (MLA_NOPE + MLA_ROPE) ** -0.5

    def attend(pos_c, ql, qr):
        s = jnp.einsum('bthc,blc->bhtl', ql, c_all) + jnp.einsum('bthr,blr->bhtl', qr, kr_all)
        s = jnp.where(key_pos[None, None, None, :] <= pos_c[None, None, :, None], s.astype(F32) * scale, -jnp.inf)
        p = jax.nn.softmax(s, -1).astype(ql.dtype)
        return jnp.einsum('bhtl,blc->bthc', p, c_all)

    o_lat = _sweep(attend, MLA_Q_BLOCK, pos, q_lat, q_rope)
    out = jnp.einsum('bthc,chv->bthv', o_lat, w_uv).reshape(b, t, MLA_WIDTH)
    return out, new_rows


def _moba_branch(q, k, v, pos, k_past, v_past):
    b, t, _ = q.shape
    qh = q.reshape(b, t, MB_KV_HEADS, MB_GROUP, MB_HEAD_DIM)
    k_new = k.reshape(b, t, MB_KV_HEADS, MB_HEAD_DIM)
    v_new = v.reshape(b, t, MB_KV_HEADS, MB_HEAD_DIM)
    k_all = jnp.concatenate([k_past.astype(k_new.dtype), k_new], 1)
    v_all = jnp.concatenate([v_past.astype(v_new.dtype), v_new], 1)
    n_len = k_all.shape[1]
    nb = -(-n_len // MB_BLOCK)
    pad = ((0, 0), (0, nb * MB_BLOCK - n_len), (0, 0), (0, 0))
    k_blk = jnp.pad(k_all, pad).reshape(b, nb, MB_BLOCK, MB_KV_HEADS, MB_HEAD_DIM)
    v_blk = jnp.pad(v_all, pad).reshape(b, nb, MB_BLOCK, MB_KV_HEADS, MB_HEAD_DIM)
    k_mean = jnp.mean(k_blk.astype(F32), axis=2).astype(q.dtype)
    topk = min(MB_TOPK, nb)
    slopes = (2.0 ** (-8.0 * jnp.arange(1, MB_HEADS + 1, dtype=F32) / MB_HEADS)).reshape(MB_KV_HEADS, MB_GROUP)
    offs = jnp.arange(MB_BLOCK)
    blk_ids = jnp.arange(nb)
    b_idx = jnp.arange(b)[:, None, None, None, None]
    kv_idx = jnp.arange(MB_KV_HEADS)[None, None, :, None, None]
    scale = MB_HEAD_DIM ** -0.5

    def attend(pos_c, qc):
        tc = pos_c.shape[0]
        qb = pos_c // MB_BLOCK
        gate = jnp.einsum('btkgd,bnkd->btkgn', qc, k_mean).astype(F32)
        past_ok = blk_ids[None, :] < qb[:, None]
        gate = jnp.where(past_ok[None, :, None, None, :], gate, -jnp.inf)
        _, idx = lax.top_k(gate, topk)
        sel_ok = idx < qb[None, :, None, None, None]
        k_sel = k_blk[b_idx, idx, :, kv_idx]
        v_sel = v_blk[b_idx, idx, :, kv_idx]
        k_own = jnp.take(k_blk, qb, axis=1)
        v_own = jnp.take(v_blk, qb, axis=1)
        dist_sel = (pos_c[None, :, None, None, None, None] - (idx[..., None] * MB_BLOCK + offs)).astype(F32)
        s_sel = (jnp.einsum('btkgd,btkgjsd->btkgjs', qc, k_sel).astype(F32) * scale
                 - slopes[None, None, :, :, None, None] * dist_sel)
        s_sel = jnp.where(sel_ok[..., None], s_sel, -jnp.inf)
        dist_own = pos_c[:, None] - (qb[:, None] * MB_BLOCK + offs[None, :])
        s_own = (jnp.einsum('btkgd,btskd->btkgs', qc, k_own).astype(F32) * scale
                 - slopes[None, None, :, :, None] * dist_own.astype(F32)[None, :, None, None, :])
        s_own = jnp.where((dist_own >= 0)[None, :, None, None, :], s_own, -jnp.inf)
        s = jnp.concatenate([s_sel.reshape(b, tc, MB_KV_HEADS, MB_GROUP, topk * MB_BLOCK), s_own], -1)
        p = jax.nn.softmax(s, -1).astype(qc.dtype)
        p_sel = p[..., :topk * MB_BLOCK].reshape(b, tc, MB_KV_HEADS, MB_GROUP, topk, MB_BLOCK)
        p_own = p[..., topk * MB_BLOCK:]
        return (jnp.einsum('btkgjs,btkgjsd->btkgd', p_sel, v_sel)
                + jnp.einsum('btkgs,btskd->btkgd', p_own, v_own))

    o = _sweep(attend, MB_Q_BLOCK, pos, qh)
    return o.reshape(b, t, MB_WIDTH), k_new, v_new


def _layer(x, pos, past_mla, past_k, past_v, wkv0, shift0, prm):
    (w_in, rw_mu, rw_w0, rw_w2, rw_a0, rw_a2, rw_k_k, rw_k_a, rw_r_k, rw_gn_g, rw_gn_b,
     mla_q_norm, mla_w_uq, mla_kv_norm, mla_w_uk, mla_w_uv, w_branch, w_out, ln_g, ln_b) = prm
    b, t, _ = x.shape
    proj = x @ w_in
    rw_cols, m_qd, m_kvd, m_kr, c_q, c_k, c_v, gate_cols, merge_cols = _split_proj(proj)
    y_a, wkv_new, shift_new = _rwkv_branch(rw_cols, shift0, wkv0, rw_mu, rw_w0, rw_w2, rw_a0, rw_a2,
                                           rw_k_k, rw_k_a, rw_r_k, rw_gn_g, rw_gn_b)
    y_b, mla_rows = _mla_branch(m_qd, m_kvd, m_kr, pos, past_mla, mla_q_norm, mla_w_uq, mla_kv_norm, mla_w_uk, mla_w_uv)
    y_c, k_new, v_new = _moba_branch(c_q, c_k, c_v, pos, past_k, past_v)
    o = jnp.stack([y_a, y_b, y_c], 2) * jax.nn.silu(gate_cols.reshape(b, t, N_BRANCH, BR_WIDTH))
    br = jnp.einsum('btnc,ncd->btnd', o, w_branch)
    merged = jnp.sum(br * jax.nn.sigmoid(merge_cols.reshape(b, t, N_BRANCH, D_MODEL)), axis=2)
    x_new = _layernorm(DEEPNORM_ALPHA * x + merged @ w_out, ln_g, ln_b)
    return x_new, mla_rows, k_new, v_new, wkv_new, shift_new


def setup_inputs(seed: int = 0) -> dict:
    key = jax.random.key(seed)
    ks = jax.random.split(key, 32)
    n_pages = PAST_LEN // PAGE_SIZE
    n_used = DEC_BATCH * n_pages
    n_pool = n_used + max(1, n_used // 4)
    nrm = lambda k, shp, s=1.0: jax.random.normal(k, shp, F32) * s
    page_table = jax.random.permutation(ks[0], n_pool)[:n_used].reshape(DEC_BATCH, n_pages).astype(jnp.int32)
    return {
        'x_prompt': nrm(ks[1], (BATCH, SEQ, D_MODEL)),
        'x_sample': nrm(ks[2], (DEC_BATCH, DEC_SEQ, D_MODEL)),
        'cache_mla': nrm(ks[3], (n_pool, DEPTH, PAGE_SIZE, MLA_ROW)),
        'cache_moba_k': nrm(ks[4], (n_pool, DEPTH, PAGE_SIZE, MB_KV_HEADS, MB_HEAD_DIM)),
        'cache_moba_v': nrm(ks[5], (n_pool, DEPTH, PAGE_SIZE, MB_KV_HEADS, MB_HEAD_DIM)),
        'state_wkv': nrm(ks[6], (DEPTH, DEC_BATCH, RW_HEADS, RW_HEAD_DIM, RW_HEAD_DIM), 0.5),
        'state_shift': nrm(ks[7], (DEPTH, DEC_BATCH, RW_SHIFT_W)),
        'page_table': page_table,
        'w_in': nrm(ks[8], (DEPTH, D_MODEL, IN_W), D_MODEL ** -0.5),
        'rw_mu': jax.random.uniform(ks[9], (DEPTH, RW_SHIFT_W), F32),
        'rw_w0': jax.random.uniform(ks[10], (DEPTH, RW_WIDTH), F32, -6.5, -1.5),
        'rw_w2': nrm(ks[11], (DEPTH, RW_DECAY_RANK, RW_WIDTH), 0.5 * RW_DECAY_RANK ** -0.5),
        'rw_a0': nrm(ks[12], (DEPTH, RW_WIDTH), 0.1),
        'rw_a2': nrm(ks[13], (DEPTH, RW_ICLR_RANK, RW_WIDTH), 0.5 * RW_ICLR_RANK ** -0.5),
        'rw_k_k': 0.85 + nrm(ks[14], (DEPTH, RW_WIDTH), 0.02),
        'rw_k_a': 1.0 + nrm(ks[15], (DEPTH, RW_WIDTH), 0.02),
        'rw_r_k': -0.04 + nrm(ks[16], (DEPTH, RW_HEADS, RW_HEAD_DIM), 0.02),
        'rw_gn_g': 1.0 + nrm(ks[17], (DEPTH, RW_WIDTH), 0.02),
        'rw_gn_b': nrm(ks[18], (DEPTH, RW_WIDTH), 0.02),
        'mla_q_norm': 1.0 + nrm(ks[19], (DEPTH, MLA_Q_RANK), 0.02),
        'mla_w_uq': nrm(ks[20], (DEPTH, MLA_Q_RANK, MLA_HEADS * (MLA_NOPE + MLA_ROPE)), MLA_Q_RANK ** -0.5),
        'mla_kv_norm': 1.0 + nrm(ks[21], (DEPTH, MLA_KV_RANK), 0.02),
        'mla_w_uk': nrm(ks[22], (DEPTH, MLA_KV_RANK, MLA_HEADS, MLA_NOPE), MLA_KV_RANK ** -0.5),
        'mla_w_uv': nrm(ks[23], (DEPTH, MLA_KV_RANK, MLA_HEADS, MLA_V), MLA_KV_RANK ** -0.5),
        'w_branch': nrm(ks[24], (DEPTH, N_BRANCH, BR_WIDTH, D_MODEL), DEEPNORM_BETA * BR_WIDTH ** -0.5),
        'w_out': nrm(ks[25], (DEPTH, D_MODEL, D_MODEL), DEEPNORM_BETA * D_MODEL ** -0.5),
        'ln_g': 1.0 + nrm(ks[26], (DEPTH, D_MODEL), 0.02),
        'ln_b': nrm(ks[27], (DEPTH, D_MODEL), 0.02),
    }


def reference(x_prompt, x_sample, cache_mla, cache_moba_k, cache_moba_v, state_wkv, state_shift, page_table,
              w_in, rw_mu, rw_w0, rw_w2, rw_a0, rw_a2, rw_k_k, rw_k_a, rw_r_k, rw_gn_g, rw_gn_b,
              mla_q_norm, mla_w_uq, mla_kv_norm, mla_w_uk, mla_w_uv, w_branch, w_out, ln_g, ln_b):
    bp, tp, _ = x_prompt.shape
    bs, ts, _ = x_sample.shape
    past_len = page_table.shape[1] * PAGE_SIZE
    pos_p = jnp.arange(tp)
    pos_s = past_len + jnp.arange(ts)
    dt = x_prompt.dtype
    empty_mla = jnp.zeros((bp, 0, MLA_ROW), dt)
    empty_kv = jnp.zeros((bp, 0, MB_KV_HEADS, MB_HEAD_DIM), dt)
    wkv_zero = jnp.zeros((bp, RW_HEADS, RW_HEAD_DIM, RW_HEAD_DIM), F32)
    shift_zero = jnp.zeros((bp, RW_SHIFT_W), dt)
    weights = (w_in, rw_mu, rw_w0, rw_w2, rw_a0, rw_a2, rw_k_k, rw_k_a, rw_r_k, rw_gn_g, rw_gn_b,
               mla_q_norm, mla_w_uq, mla_kv_norm, mla_w_uk, mla_w_uv, w_branch, w_out, ln_g, ln_b)
    xp, xs = x_prompt, x_sample
    mla_p, mla_s, kp, ksm, vp, vsm, wp, wsm, sp, ssm = [], [], [], [], [], [], [], [], [], []
    for l in range(DEPTH):
        prm = tuple(wt[l] for wt in weights)
        xp, r_m, r_k, r_v, r_w, r_s = _layer(xp, pos_p, empty_mla, empty_kv, empty_kv, wkv_zero, shift_zero, prm)
        mla_p.append(r_m); kp.append(r_k); vp.append(r_v); wp.append(r_w); sp.append(r_s)
        past_mla = cache_mla[page_table, l].reshape(bs, past_len, MLA_ROW)
        past_k = cache_moba_k[page_table, l].reshape(bs, past_len, MB_KV_HEADS, MB_HEAD_DIM)
        past_v = cache_moba_v[page_table, l].reshape(bs, past_len, MB_KV_HEADS, MB_HEAD_DIM)
        xs, r_m, r_k, r_v, r_w, r_s = _layer(xs, pos_s, past_mla, past_k, past_v, state_wkv[l], state_shift[l], prm)
        mla_s.append(r_m); ksm.append(r_k); vsm.append(r_v); wsm.append(r_w); ssm.append(r_s)
    mla_rows_prompt = jnp.stack(mla_p, 1)
    mla_rows_sample = jnp.stack(mla_s, 1)
    moba_k_prompt = jnp.stack(kp, 1)
    moba_k_sample = jnp.stack(ksm, 1)
    moba_v_prompt = jnp.stack(vp, 1)
    moba_v_sample = jnp.stack(vsm, 1)
    wkv_prompt = jnp.stack(wp, 0).astype(dt)
    wkv_sample = jnp.stack(wsm, 0).astype(x_sample.dtype)
    shift_prompt = jnp.stack(sp, 0)
    shift_sample = jnp.stack(ssm, 0)
    return (xp, xs, mla_rows_prompt, mla_rows_sample, moba_k_prompt, moba_k_sample, moba_v_prompt, moba_v_sample,
            wkv_prompt, wkv_sample, shift_prompt, shift_sample)
```

```python
import functools
import math

import jax
import jax.numpy as jnp
from jax import lax
from jax.experimental import pallas as pl
from jax.experimental.pallas import tpu as pltpu

F32 = jnp.float32
BF16 = jnp.bfloat16
HIGHEST = lax.Precision.HIGHEST

D_MODEL = 1024
PAGE_SIZE = 128
RW_HEADS = 8
RW_HEAD_DIM = 64
RW_WIDTH = 512
RW_DECAY_RANK = 64
RW_SHIFT_W = 1664
RW_GN_EPS = 64e-5
MLA_HEADS = 8
MLA_NOPE = 64
MLA_ROPE = 32
MLA_V = 64
MLA_Q_RANK = 256
MLA_KV_RANK = 128
MLA_ROW = 160
ROPE_THETA = 10000.0
MB_HEADS = 8
MB_KV_HEADS = 2
MB_HEAD_DIM = 64
MB_GROUP = 4
MB_BLOCK = 256
MB_TOPK = 3
N_BRANCH = 3
BR_WIDTH = 512
NEG_BIG = -1e30

OFF_RW = 0
OFF_KV = 1792
OFF_MLA = 2048
OFF_MQ = 2560
OFF_MERGE = 3072
OFF_GATE = 6144
PROJ_W = 7680

VMEM_LIMIT = 56 * 1024 * 1024


def _cparams(sem, vmem=VMEM_LIMIT):
    return pltpu.CompilerParams(dimension_semantics=sem, vmem_limit_bytes=vmem)


def _pick(n, prefs):
    for p in prefs:
        if n % p == 0:
            return p
    return n


def _dot(a, b, **kw):
    return jnp.dot(a, b, preferred_element_type=F32, **kw)


def _dot_nt(a, b, **kw):
    return lax.dot_general(a, b, (((1,), (1,)), ((), ())), preferred_element_type=F32, **kw)


def _sigmoid(x):
    return 1.0 / (1.0 + jnp.exp(-x))


def _inproj_kernel(x_ref, w_ref, o_ref, xb_ref):
    @pl.when(pl.program_id(1) == 0)
    def _():
        xb_ref[...] = x_ref[...].astype(BF16)

    o_ref[...] = _dot(xb_ref[...], w_ref[...])


def _inproj(x, w_perm):
    m, d = x.shape
    tm = _pick(m, (512, 256, 128))
    tn = 1280
    return pl.pallas_call(
        _inproj_kernel,
        grid=(m // tm, PROJ_W // tn),
        in_specs=[pl.BlockSpec((tm, d), lambda i, j: (i, 0)),
                  pl.BlockSpec((d, tn), lambda i, j: (0, j))],
        out_specs=pl.BlockSpec((tm, tn), lambda i, j: (i, j)),
        out_shape=jax.ShapeDtypeStruct((m, PROJ_W), F32),
        scratch_shapes=[pltpu.VMEM((tm, d), BF16)],
        compiler_params=_cparams(("parallel", "arbitrary")),
        name="inproj",
    )(x, w_perm)


def _rwkv_prep(cols, prev, mu, w0, w2b, a0, a2b, k_k, k_a):
    mixed = cols + (prev - cols) * mu
    r = mixed[:, 0:512]
    k = mixed[:, 512:1024]
    v = mixed[:, 1024:1536]
    wd = mixed[:, 1536:1600]
    ad = mixed[:, 1600:1664]
    zw = w0 + _dot(jnp.tanh(wd).astype(BF16), w2b)
    nz = -zw
    softplus = jnp.maximum(nz, 0.0) + jnp.log(1.0 + jnp.exp(-jnp.abs(nz)))
    lw = -jnp.exp(-softplus - 0.5)
    a = _sigmoid(a0 + _dot(ad.astype(BF16), a2b))
    kk = k * k_k
    k_mod = k * (1.0 + (a - 1.0) * k_a)
    return r, k_mod, v, lw, a, kk


def _head_norm(kk_h):
    ss = jnp.sum(kk_h * kk_h, axis=-1, keepdims=True)
    return kk_h * lax.rsqrt(jnp.maximum(ss, 1e-24))


def _group_norm_bonus(y_h, r_h, kmod_h, v_h, rk_h, g_h, b_h):
    ym = jnp.mean(y_h, axis=-1, keepdims=True)
    yc = y_h - ym
    yv = jnp.mean(yc * yc, axis=-1, keepdims=True)
    yn = yc * lax.rsqrt(yv + RW_GN_EPS) * g_h + b_h
    bonus = jnp.sum(r_h * kmod_h * rk_h, axis=-1, keepdims=True) * v_h
    return yn + bonus


def _rwkv_chunk_kernel(p_ref, shift0_ref, wkv0_ref, mu_ref, w0_ref, w2_ref, a0_ref, a2_ref,
                       kk_ref, ka_ref, rk_ref, gg_ref, gb_ref,
                       y_ref, wkv_ref, carry_ref, s_ref, *, chunk):
    c = pl.program_id(1)
    nc = pl.num_programs(1)
    C = chunk

    @pl.when(c == 0)
    def _():
        carry_ref[...] = shift0_ref[0]
        s_ref[...] = wkv0_ref[0]

    cols = p_ref[...]
    rows = lax.broadcasted_iota(jnp.int32, cols.shape, 0)
    prev = jnp.where(rows == 0, carry_ref[...], pltpu.roll(cols, 1, 0))
    carry_ref[...] = cols[C - 1:C, :]

    r, k_mod, v, lw, a, kk = _rwkv_prep(cols, prev, mu_ref[...], w0_ref[...], w2_ref[...],
                                        a0_ref[...], a2_ref[...], kk_ref[...], ka_ref[...])

    ti = lax.broadcasted_iota(jnp.int32, (C, C), 0)
    si = lax.broadcasted_iota(jnp.int32, (C, C), 1)
    lower = ti >= si
    strict = ti > si
    eye = (ti == si).astype(F32)
    cum = _dot(lower.astype(F32), lw, precision=HIGHEST)
    p_incl = jnp.exp(cum)
    p_excl = jnp.exp(cum - lw)
    p_inv = jnp.exp(-cum)
    p_end = p_incl[C - 1:C, :]

    n_sq = int(math.log2(C)) - 1
    for h in range(RW_HEADS):
        sl = slice(h * 64, (h + 1) * 64)
        kk_h = _head_norm(kk[:, sl])
        a_h = a[:, sl]
        v_h = v[:, sl]
        r_h = r[:, sl]
        kmod_h = k_mod[:, sl]
        at = -(kk_h * p_excl[:, sl])
        bt = kk_h * a_h * p_inv[:, sl]
        kt = kmod_h * p_inv[:, sl]
        rt = r_h * p_incl[:, sl]
        s0 = s_ref[h]

        lhs = jnp.concatenate([at, rt], axis=0)
        rhs = jnp.concatenate([bt, kt], axis=0)
        mx = _dot_nt(lhs, rhs, precision=HIGHEST)
        a_ab = jnp.where(strict, mx[:C, :C], 0.0)
        a_ak = jnp.where(strict, mx[:C, C:], 0.0)
        r_b = jnp.where(lower, mx[C:, :C], 0.0)
        r_k = jnp.where(lower, mx[C:, C:], 0.0)

        inv = eye + a_ab
        npow = a_ab
        for _ in range(n_sq):
            npow = _dot(npow, npow, precision=HIGHEST)
            inv = inv + _dot(npow, inv, precision=HIGHEST)

        sa = _dot_nt(lhs, s0, precision=HIGHEST)
        rhs_u = sa[:C] + _dot(a_ak, v_h, precision=HIGHEST)
        u = _dot(inv, rhs_u, precision=HIGHEST)
        y_h = sa[C:] + _dot(r_b, u, precision=HIGHEST) + _dot(r_k, v_h, precision=HIGHEST)

        pe = p_end[:, sl]
        uv_t = jnp.concatenate([u, v_h], axis=1).T
        bk = jnp.concatenate([bt * pe, kt * pe], axis=1)
        upd = _dot(uv_t, bk, precision=HIGHEST)
        s_ref[h] = s0 * pe + upd[:64, :64] + upd[64:, 64:]

        y_ref[:, sl] = _group_norm_bonus(y_h, r_h, kmod_h, v_h, rk_ref[:, sl], gg_ref[:, sl], gb_ref[:, sl])

    @pl.when(c == nc - 1)
    def _():
        wkv_ref[0] = s_ref[...]


def _rwkv_prompt(proj, shift0, wkv0, rw, nb, t):
    chunk = 128
    assert t % chunk == 0
    nc = t // chunk
    vec = lambda n: pl.BlockSpec((1, n), lambda b, c: (0, 0))
    mat = lambda a, b_: pl.BlockSpec((a, b_), lambda b, c: (0, 0))
    return pl.pallas_call(
        functools.partial(_rwkv_chunk_kernel, chunk=chunk),
        grid=(nb, nc),
        in_specs=[pl.BlockSpec((chunk, RW_SHIFT_W), lambda b, c: (b * nc + c, 0)),
                  pl.BlockSpec((1, 1, RW_SHIFT_W), lambda b, c: (b, 0, 0)),
                  pl.BlockSpec((1, RW_HEADS, 64, 64), lambda b, c: (b, 0, 0, 0)),
                  vec(RW_SHIFT_W), vec(512), mat(64, 512), vec(512), mat(64, 512),
                  vec(512), vec(512), vec(512), vec(512), vec(512)],
        out_specs=[pl.BlockSpec((chunk, 512), lambda b, c: (b * nc + c, 0)),
                   pl.BlockSpec((1, RW_HEADS, 64, 64), lambda b, c: (b, 0, 0, 0))],
        out_shape=[jax.ShapeDtypeStruct((nb * t, 512), F32),
                   jax.ShapeDtypeStruct((nb, RW_HEADS, 64, 64), F32)],
        scratch_shapes=[pltpu.VMEM((1, RW_SHIFT_W), F32), pltpu.VMEM((RW_HEADS, 64, 64), F32)],
        compiler_params=_cparams(("parallel", "arbitrary")),
        name="rwkv_prompt",
    )(proj, shift0.reshape(nb, 1, RW_SHIFT_W), wkv0, *rw)


def _rwkv_step_kernel(p_ref, shift0_ref, wkv0_ref, mu_ref, w0_ref, w2_ref, a0_ref, a2_ref,
                      kk_ref, ka_ref, rk_ref, gg_ref, gb_ref,
                      y_ref, wkv_ref, r_s, w_s, k_s, kk_s, b_s, v_s, vt_s, yt_s, *, bt):
    i = pl.program_id(0)
    n = pl.num_programs(0)
    nb = p_ref.shape[0]

    @pl.when(i == 0)
    def _():
        r, k_mod, v, lw, a, kk = _rwkv_prep(p_ref[:, :RW_SHIFT_W], shift0_ref[...], mu_ref[...], w0_ref[...],
                                            w2_ref[...], a0_ref[...], a2_ref[...], kk_ref[...], ka_ref[...])
        r_s[...] = r
        w_s[...] = jnp.exp(lw)
        k_s[...] = k_mod
        v_s[...] = v
        vt_s[...] = v.T
        for h in range(RW_HEADS):
            sl = slice(h * 64, (h + 1) * 64)
            kk_h = _head_norm(kk[:, sl])
            kk_s[:, sl] = kk_h
            b_s[:, sl] = kk_h * a[:, sl]
        yt_s[...] = jnp.zeros_like(yt_s)

    lane = lax.broadcasted_iota(jnp.int32, (64, nb), 1)
    for j in range(bt):
        b = i * bt + j
        r_row = r_s[pl.ds(b, 1), :]
        w_row = w_s[pl.ds(b, 1), :]
        k_row = k_s[pl.ds(b, 1), :]
        kk_row = kk_s[pl.ds(b, 1), :]
        b_row = b_s[pl.ds(b, 1), :]
        onb = lane == b
        for h in range(RW_HEADS):
            sl = slice(h * 64, (h + 1) * 64)
            s0 = wkv0_ref[j, h]
            sa = -jnp.sum(s0 * kk_row[:, sl], axis=-1, keepdims=True)
            v_col = jnp.sum(jnp.where(onb, vt_s[sl, :], 0.0), axis=-1, keepdims=True)
            s1 = s0 * w_row[:, sl] + sa * b_row[:, sl] + v_col * k_row[:, sl]
            wkv_ref[j, h] = s1
            y_col = jnp.sum(s1 * r_row[:, sl], axis=-1, keepdims=True)
            yt_s[sl, :] = yt_s[sl, :] + jnp.where(onb, y_col, 0.0)

    @pl.when(i == n - 1)
    def _():
        y = yt_s[...].T
        for h in range(RW_HEADS):
            sl = slice(h * 64, (h + 1) * 64)
            y_ref[:, sl] = _group_norm_bonus(y[:, sl], r_s[:, sl], k_s[:, sl], v_s[:, sl],
                                             rk_ref[:, sl], gg_ref[:, sl], gb_ref[:, sl])


def _rwkv_sample(proj, shift0, wkv0, rw):
    nb = proj.shape[0]
    bt = 8
    assert nb % bt == 0
    vec = lambda n: pl.BlockSpec((1, n), lambda i: (0, 0))
    mat = lambda a, b_: pl.BlockSpec((a, b_), lambda i: (0, 0))
    sq = pltpu.VMEM((nb, 512), F32)
    return pl.pallas_call(
        functools.partial(_rwkv_step_kernel, bt=bt),
        grid=(nb // bt,),
        in_specs=[pl.BlockSpec((nb, 1792), lambda i: (0, 0)),
                  mat(nb, RW_SHIFT_W),
                  pl.BlockSpec((bt, RW_HEADS, 64, 64), lambda i: (i, 0, 0, 0)),
                  vec(RW_SHIFT_W), vec(512), mat(64, 512), vec(512), mat(64, 512),
                  vec(512), vec(512), vec(512), vec(512), vec(512)],
        out_specs=[pl.BlockSpec((nb, 512), lambda i: (0, 0)),
                   pl.BlockSpec((bt, RW_HEADS, 64, 64), lambda i: (i, 0, 0, 0))],
        out_shape=[jax.ShapeDtypeStruct((nb, 512), F32),
                   jax.ShapeDtypeStruct((nb, RW_HEADS, 64, 64), F32)],
        scratch_shapes=[sq, sq, sq, sq, sq, sq, pltpu.VMEM((512, nb), F32), pltpu.VMEM((512, nb), F32)],
        compiler_params=_cparams(("arbitrary",)),
        name="rwkv_sample",
    )(proj, shift0, wkv0, *rw)


def _rms(x, g):
    return x * lax.rsqrt(jnp.mean(x * x, axis=-1, keepdims=True) + 1e-6) * g


def _mla_prep_kernel(p_ref, cos_ref, sin_ref, qn_ref, wuq_ref, kvn_ref, wukt_ref,
                     rows_ref, rowsb_ref, qf_ref):
    blk = p_ref[...]
    tm = blk.shape[0]
    qd = blk[:, 0:256]
    kvd = blk[:, 256:384]
    kr = blk[:, 384:416]
    q = _dot(_rms(qd, qn_ref[...]).astype(BF16), wuq_ref[...])
    cos = cos_ref[...]
    sin = sin_ref[...]
    x1 = q[:, 512:640]
    x2 = q[:, 640:768]
    r1 = x1 * cos - x2 * sin
    r2 = x2 * cos + x1 * sin
    c_kv = _rms(kvd, kvn_ref[...])
    c16 = cos[:, :16]
    s16 = sin[:, :16]
    k1 = kr[:, :16]
    k2 = kr[:, 16:32]
    k_rope = jnp.concatenate([k1 * c16 - k2 * s16, k2 * c16 + k1 * s16], axis=-1)
    rows_ref[...] = jnp.concatenate([c_kv, k_rope], axis=-1)
    pad = jnp.zeros((tm, 96), F32)
    rowsb_ref[...] = jnp.concatenate([c_kv, k_rope, pad], axis=-1).astype(BF16)
    scale = (MLA_NOPE + MLA_ROPE) ** -0.5
    for h in range(MLA_HEADS):
        ql = _dot(q[:, h * 64:(h + 1) * 64].astype(BF16), wukt_ref[h])
        qr = jnp.concatenate([r1[:, h * 16:(h + 1) * 16], r2[:, h * 16:(h + 1) * 16]], axis=-1)
        qf_ref[h] = (jnp.concatenate([ql, qr, pad], axis=-1) * scale).astype(BF16)


def _mla_prep(proj, cos, sin, mw):
    m = proj.shape[0]
    tm = _pick(m, (256, 128))
    q_norm, wuq, kv_norm, wukt = mw
    return pl.pallas_call(
        _mla_prep_kernel,
        grid=(m // tm,),
        in_specs=[pl.BlockSpec((tm, 512), lambda i: (i, OFF_MLA // 512)),
                  pl.BlockSpec((tm, 128), lambda i: (i, 0)),
                  pl.BlockSpec((tm, 128), lambda i: (i, 0)),
                  pl.BlockSpec((1, 256), lambda i: (0, 0)),
                  pl.BlockSpec((256, 768), lambda i: (0, 0)),
                  pl.BlockSpec((1, 128), lambda i: (0, 0)),
                  pl.BlockSpec((MLA_HEADS, 64, 128), lambda i: (0, 0, 0))],
        out_specs=[pl.BlockSpec((tm, MLA_ROW), lambda i: (i, 0)),
                   pl.BlockSpec((tm, 256), lambda i: (i, 0)),
                   pl.BlockSpec((MLA_HEADS, tm, 256), lambda i: (0, i, 0))],
        out_shape=[jax.ShapeDtypeStruct((m, MLA_ROW), F32),
                   jax.ShapeDtypeStruct((m, 256), BF16),
                   jax.ShapeDtypeStruct((MLA_HEADS, m, 256), BF16)],
        compiler_params=_cparams(("parallel",)),
        name="mla_prep",
    )(proj, cos, sin, q_norm, wuq, kv_norm, wukt)


def _mla_attn_kernel(q_ref, k_ref, wuv_ref, y_ref, m_ref, l_ref, acc_ref, *, tq):
    i = pl.program_id(1)
    j = pl.program_id(2)
    nh = MLA_HEADS

    @pl.when(j == 0)
    def _():
        m_ref[...] = jnp.full_like(m_ref, NEG_BIG)
        l_ref[...] = jnp.zeros_like(l_ref)
        acc_ref[...] = jnp.zeros_like(acc_ref)

    @pl.when(j <= i)
    def _():
        q = q_ref[...].reshape(nh * tq, 256)
        k = k_ref[...]
        s = _dot_nt(q, k)
        row = lax.broadcasted_iota(jnp.int32, s.shape, 0) % tq
        col = lax.broadcasted_iota(jnp.int32, s.shape, 1)
        s = jnp.where(col <= row + (i - j) * tq, s, NEG_BIG)
        m_old = m_ref[...]
        m_new = jnp.maximum(m_old, jnp.max(s, axis=-1, keepdims=True))
        alpha = jnp.exp(m_old - m_new)
        p = jnp.exp(s - m_new)
        l_ref[...] = alpha * l_ref[...] + jnp.sum(p, axis=-1, keepdims=True)
        acc_ref[...] = alpha * acc_ref[...] + _dot(p.astype(BF16), k[:, :MLA_KV_RANK])
        m_ref[...] = m_new

    @pl.when(j == i)
    def _():
        o = acc_ref[...] / l_ref[...]
        for h in range(nh):
            y_ref[:, h * 64:(h + 1) * 64] = _dot(o[h * tq:(h + 1) * tq].astype(BF16), wuv_ref[h])


def _mla_attn_prompt(qf, rowsb, wuv, nb, t):
    tq = _pick(t, (256, 128))
    nq = t // tq
    return pl.pallas_call(
        functools.partial(_mla_attn_kernel, tq=tq),
        grid=(nb, nq, nq),
        in_specs=[pl.BlockSpec((MLA_HEADS, tq, 256), lambda b, i, j: (0, b * nq + i, 0)),
                  pl.BlockSpec((tq, 256), lambda b, i, j: (b * nq + jnp.minimum(i, j), 0)),
                  pl.BlockSpec((MLA_HEADS, 128, 64), lambda b, i, j: (0, 0, 0))],
        out_specs=pl.BlockSpec((tq, 512), lambda b, i, j: (b * nq + i, 0)),
        out_shape=jax.ShapeDtypeStruct((nb * t, 512), F32),
        scratch_shapes=[pltpu.VMEM((MLA_HEADS * tq, 1), F32), pltpu.VMEM((MLA_HEADS * tq, 1), F32),
                        pltpu.VMEM((MLA_HEADS * tq, MLA_KV_RANK), F32)],
        compiler_params=_cparams(("parallel", "parallel", "arbitrary")),
        name="mla_attn_prompt",
    )(qf, rowsb, wuv)


def _page_copy(cache_ref, buf_ref, sem_ref, row, slot, j):
    return pltpu.make_async_copy(cache_ref.at[row], buf_ref.at[slot, j], sem_ref.at[slot])


def _stream_step(pt_ref, cache_ref, buf_ref, sem_ref, *, layer, depth, pc):
    b = pl.program_id(0)
    c = pl.program_id(1)
    nb = pl.num_programs(0)
    nc = pl.num_programs(1)
    step = b * nc + c
    slot = step % 2

    def fetch(bb, cc, sl):
        for j in range(pc):
            row = pt_ref[bb, cc * pc + j] * depth + layer
            _page_copy(cache_ref, buf_ref, sem_ref, row, sl, j).start()

    @pl.when(step == 0)
    def _():
        fetch(b, c, slot)

    @pl.when(step + 1 < nb * nc)
    def _():
        wrap = c + 1 == nc
        fetch(jnp.where(wrap, b + 1, b), jnp.where(wrap, 0, c + 1), 1 - slot)

    for j in range(pc):
        _page_copy(cache_ref, buf_ref, sem_ref, 0, slot, j).wait()
    return slot


def _mla_decode_kernel(pt_ref, q_ref, new_ref, cache_ref, o_ref, buf_ref, sem_ref, m_ref, l_ref, acc_ref,
                       *, layer, depth, pc):
    c = pl.program_id(1)
    nc = pl.num_programs(1)
    slot = _stream_step(pt_ref, cache_ref, buf_ref, sem_ref, layer=layer, depth=depth, pc=pc)

    @pl.when(c == 0)
    def _():
        m_ref[...] = jnp.full_like(m_ref, NEG_BIG)
        l_ref[...] = jnp.zeros_like(l_ref)
        acc_ref[...] = jnp.zeros_like(acc_ref)

    q = q_ref[0]
    ql = q[:, :MLA_KV_RANK]
    qr = q[:, MLA_KV_RANK:MLA_ROW]
    rows = buf_ref[slot].reshape(pc * PAGE_SIZE, MLA_ROW)
    kc = rows[:, :MLA_KV_RANK].astype(BF16)
    kr = rows[:, MLA_KV_RANK:].astype(BF16)
    s = _dot_nt(ql, kc) + _dot_nt(qr, kr)
    m_old = m_ref[...]
    m_new = jnp.maximum(m_old, jnp.max(s, axis=-1, keepdims=True))
    alpha = jnp.exp(m_old - m_new)
    p = jnp.exp(s - m_new)
    l_new = alpha * l_ref[...] + jnp.sum(p, axis=-1, keepdims=True)
    acc_new = alpha * acc_ref[...] + _dot(p.astype(BF16), kc)
    m_ref[...] = m_new
    l_ref[...] = l_new
    acc_ref[...] = acc_new

    @pl.when(c == nc - 1)
    def _():
        new = new_ref[0].astype(F32)
        s_n = jnp.sum(q.astype(F32) * new, axis=-1, keepdims=True)
        m_f = jnp.maximum(m_new, s_n)
        al = jnp.exp(m_new - m_f)
        p_n = jnp.exp(s_n - m_f)
        l_f = al * l_new + p_n
        acc_f = al * acc_new + p_n.astype(BF16).astype(F32) * new[:, :MLA_KV_RANK]
        o_ref[0] = acc_f / l_f


def _mla_decode(page_table, qf_s, rowsb_s, cache, layer, depth):
    nb, n_pages = page_table.shape
    pc = _pick(n_pages, (16, 8, 4, 2, 1))
    nc = n_pages // pc
    grid_spec = pltpu.PrefetchScalarGridSpec(
        num_scalar_prefetch=1,
        grid=(nb, nc),
        in_specs=[pl.BlockSpec((1, MLA_HEADS, 256), lambda b, c, pt: (b, 0, 0)),
                  pl.BlockSpec((1, 1, 256), lambda b, c, pt: (b, 0, 0)),
                  pl.BlockSpec(memory_space=pl.ANY)],
        out_specs=pl.BlockSpec((1, MLA_HEADS, MLA_KV_RANK), lambda b, c, pt: (b, 0, 0)),
        scratch_shapes=[pltpu.VMEM((2, pc, PAGE_SIZE, MLA_ROW), F32),
                        pltpu.SemaphoreType.DMA((2,)),
                        pltpu.VMEM((MLA_HEADS, 1), F32), pltpu.VMEM((MLA_HEADS, 1), F32),
                        pltpu.VMEM((MLA_HEADS, MLA_KV_RANK), F32)])
    return pl.pallas_call(
        functools.partial(_mla_decode_kernel, layer=layer, depth=depth, pc=pc),
        grid_spec=grid_spec,
        out_shape=jax.ShapeDtypeStruct((nb, MLA_HEADS, MLA_KV_RANK), F32),
        compiler_params=_cparams(("arbitrary", "arbitrary")),
        name="mla_decode",
    )(page_table, qf_s, rowsb_s.reshape(nb, 1, 256), cache)


def _mla_out_kernel(o_ref, wuv_ref, y_ref):
    for h in range(MLA_HEADS):
        y_ref[:, h * 64:(h + 1) * 64] = _dot(o_ref[h].astype(BF16), wuv_ref[h])


def _mla_out(o_hm, wuv):
    m = o_hm.shape[1]
    return pl.pallas_call(
        _mla_out_kernel,
        grid=(1,),
        in_specs=[pl.BlockSpec((MLA_HEADS, m, 128), lambda i: (0, 0, 0)),
                  pl.BlockSpec((MLA_HEADS, 128, 64), lambda i: (0, 0, 0))],
        out_specs=pl.BlockSpec((m, 512), lambda i: (0, 0)),
        out_shape=jax.ShapeDtypeStruct((m, 512), F32),
        compiler_params=_cparams(("arbitrary",)),
        name="mla_out",
    )(o_hm, wuv)


def _topk_select(gate, gate_row, n_valid, topk):
    nblk = gate.shape[0]
    blk = lax.broadcasted_iota(jnp.int32, gate.shape, 0)
    rank = jnp.zeros(gate.shape, F32)
    for m in range(nblk):
        g_m = gate_row(m)
        ahead = jnp.where(g_m > gate, 1.0, jnp.where((g_m == gate) & (m < blk), 1.0, 0.0))
        rank = rank + ahead * jnp.where(m < n_valid, 1.0, 0.0)
    return jnp.where((rank < topk) & (blk < n_valid), 1.0, 0.0)


def _moba_prompt_kernel(q_ref, kv_ref, slope_ref, y_ref, vt_ref, km_ref, gate_ref, sel_ref, *, nblk):
    i = pl.program_id(1)
    nq = MB_GROUP * MB_BLOCK
    scale = MB_HEAD_DIM ** -0.5

    @pl.when(i == 0)
    def _():
        for n in range(nblk):
            blk = kv_ref[n * MB_BLOCK:(n + 1) * MB_BLOCK, :]
            vt_ref[n] = blk[:, 128:256].T
            km_ref[n:n + 1, :] = jnp.mean(blk[:, 0:128], axis=0, keepdims=True)

    q = q_ref[...]
    lane = lax.broadcasted_iota(jnp.int32, (1, nq), 1)
    qpos = (i * MB_BLOCK + lane % MB_BLOCK).astype(F32)
    koff = lax.broadcasted_iota(jnp.int32, (MB_BLOCK, 1), 0).astype(F32)
    blk_id = lax.broadcasted_iota(jnp.int32, (nblk, nq), 0)
    outs = []
    for g in range(MB_KV_HEADS):
        qs = jnp.concatenate([q[:, (g * MB_GROUP + e) * 64:(g * MB_GROUP + e + 1) * 64]
                              for e in range(MB_GROUP)], axis=0)
        qsb = (qs * scale).astype(BF16)
        slope = slope_ref[g:g + 1, :]
        gate = _dot_nt(km_ref[:, g * 64:(g + 1) * 64], qs, precision=HIGHEST)
        gate_ref[...] = gate
        sel_ref[...] = _topk_select(gate, lambda m: gate_ref[m:m + 1, :], i, MB_TOPK)

        def scores(n):
            kb = kv_ref[pl.ds(pl.multiple_of(n * MB_BLOCK, MB_BLOCK), MB_BLOCK), g * 64:(g + 1) * 64]
            s = _dot_nt(kb.astype(BF16), qsb)
            kpos = (n * MB_BLOCK).astype(F32) + koff
            return s - slope * (qpos - kpos), kpos

        def accumulate(n, s, carry):
            m_old, l_old, acc = carry
            m_new = jnp.maximum(m_old, jnp.max(s, axis=0, keepdims=True))
            alpha = jnp.exp(m_old - m_new)
            p = jnp.exp(s - m_new)
            l_new = alpha * l_old + jnp.sum(p, axis=0, keepdims=True)
            vt = vt_ref[n][g * 64:(g + 1) * 64, :]
            acc = alpha * acc + _dot(vt.astype(BF16), p.astype(BF16))
            return m_new, l_new, acc

        s_own, kpos = scores(i)
        s_own = jnp.where(kpos <= qpos, s_own, NEG_BIG)
        init = (jnp.full((1, nq), NEG_BIG, F32), jnp.zeros((1, nq), F32), jnp.zeros((64, nq), F32))
        carry = accumulate(i, s_own, init)

        def body(n, carry):
            s, _ = scores(n)
            s = jnp.where(sel_ref[pl.ds(n, 1), :] > 0.5, s, NEG_BIG)
            return accumulate(n, s, carry)

        _, l_f, acc = lax.fori_loop(0, i, body, carry)
        outs.append(acc / l_f)

    ot = jnp.concatenate(outs, axis=0).T
    for g in range(MB_KV_HEADS):
        for e in range(MB_GROUP):
            hh = g * MB_GROUP + e
            y_ref[:, hh * 64:(hh + 1) * 64] = ot[e * MB_BLOCK:(e + 1) * MB_BLOCK, g * 64:(g + 1) * 64]


def _moba_prompt(proj, slopes_row, nb, t):
    assert t % MB_BLOCK == 0
    nblk = t // MB_BLOCK
    return pl.pallas_call(
        functools.partial(_moba_prompt_kernel, nblk=nblk),
        grid=(nb, nblk),
        in_specs=[pl.BlockSpec((MB_BLOCK, 512), lambda b, i: (b * nblk + i, OFF_MQ // 512)),
                  pl.BlockSpec((t, 256), lambda b, i: (b, OFF_KV // 256)),
                  pl.BlockSpec((MB_KV_HEADS, MB_GROUP * MB_BLOCK), lambda b, i: (0, 0))],
        out_specs=pl.BlockSpec((MB_BLOCK, 512), lambda b, i: (b * nblk + i, 0)),
        out_shape=jax.ShapeDtypeStruct((nb * t, 512), F32),
        scratch_shapes=[pltpu.VMEM((nblk, 128, MB_BLOCK), F32), pltpu.VMEM((nblk, 128), F32),
                        pltpu.VMEM((nblk, MB_GROUP * MB_BLOCK), F32), pltpu.VMEM((nblk, MB_GROUP * MB_BLOCK), F32)],
        compiler_params=_cparams(("parallel", "arbitrary")),
        name="moba_prompt",
    )(proj, proj, slopes_row)


def _moba_keys_kernel(pt_ref, q_ref, knew_ref, slope_ref, cache_ref, p_ref, pown_ref,
                      buf_ref, sem_ref, s_ref, ksum_ref, *, layer, depth, pc, past_len):
    c = pl.program_id(1)
    nc = pl.num_programs(1)
    slot = _stream_step(pt_ref, cache_ref, buf_ref, sem_ref, layer=layer, depth=depth, pc=pc)
    scale = MB_HEAD_DIM ** -0.5
    q = q_ref[0]
    qb = (q * scale).astype(BF16)
    keys = buf_ref[slot].reshape(pc * PAGE_SIZE, 128)
    s_ref[c] = _dot_nt(qb, keys.astype(BF16))
    bpc = pc * PAGE_SIZE // MB_BLOCK
    for n in range(bpc):
        ksum_ref[c, n:n + 1, :] = jnp.sum(keys[n * MB_BLOCK:(n + 1) * MB_BLOCK, :], axis=0, keepdims=True)

    @pl.when(c == nc - 1)
    def _():
        nblk = nc * bpc
        kmean = ksum_ref[...].reshape(nblk, 128) * (1.0 / MB_BLOCK)
        gate_t = _dot_nt(kmean, q, precision=HIGHEST)
        sel_t = _topk_select(gate_t, lambda m: gate_t[m:m + 1, :], nblk, MB_TOPK)
        eye = (lax.broadcasted_iota(jnp.int32, (MB_HEADS, MB_HEADS), 0)
               == lax.broadcasted_iota(jnp.int32, (MB_HEADS, MB_HEADS), 1)).astype(F32)
        sel = _dot_nt(eye, sel_t)
        slope = slope_ref[...]
        off = lax.broadcasted_iota(jnp.int32, (1, MB_BLOCK), 1)
        s_own = jnp.sum(qb.astype(F32) * knew_ref[0].astype(BF16).astype(F32), axis=-1, keepdims=True)

        def masked(cc, n):
            blk = cc * bpc + n
            dist = (past_len - blk * MB_BLOCK - off).astype(F32)
            s_blk = s_ref[cc][:, n * MB_BLOCK:(n + 1) * MB_BLOCK] - slope * dist
            return jnp.where(sel[:, blk:blk + 1] > 0.5, s_blk, NEG_BIG)

        m = s_own
        for cc in range(nc):
            for n in range(bpc):
                m = jnp.maximum(m, jnp.max(masked(cc, n), axis=-1, keepdims=True))
        l = jnp.exp(s_own - m)
        for cc in range(nc):
            for n in range(bpc):
                l = l + jnp.sum(jnp.exp(masked(cc, n) - m), axis=-1, keepdims=True)
        inv = 1.0 / l
        for cc in range(nc):
            for n in range(bpc):
                p_ref[0, cc, :, n * MB_BLOCK:(n + 1) * MB_BLOCK] = jnp.exp(masked(cc, n) - m) * inv
        pown_ref[0] = jnp.broadcast_to(jnp.exp(s_own - m) * inv, (MB_HEADS, 128))


def _moba_keys(page_table, q_pad, k_new, slopes_col, cache, layer, depth):
    nb, n_pages = page_table.shape
    pc = _pick(n_pages, (16, 8, 4, 2))
    nc = n_pages // pc
    past_len = n_pages * PAGE_SIZE
    bpc = pc * PAGE_SIZE // MB_BLOCK
    grid_spec = pltpu.PrefetchScalarGridSpec(
        num_scalar_prefetch=1,
        grid=(nb, nc),
        in_specs=[pl.BlockSpec((1, MB_HEADS, 128), lambda b, c, pt: (b, 0, 0)),
                  pl.BlockSpec((1, 1, 128), lambda b, c, pt: (b, 0, 0)),
                  pl.BlockSpec((MB_HEADS, 1), lambda b, c, pt: (0, 0)),
                  pl.BlockSpec(memory_space=pl.ANY)],
        out_specs=[pl.BlockSpec((1, nc, MB_HEADS, pc * PAGE_SIZE), lambda b, c, pt: (b, 0, 0, 0)),
                   pl.BlockSpec((1, MB_HEADS, 128), lambda b, c, pt: (b, 0, 0))],
        scratch_shapes=[pltpu.VMEM((2, pc, PAGE_SIZE, 128), F32),
                        pltpu.SemaphoreType.DMA((2,)),
                        pltpu.VMEM((nc, MB_HEADS, pc * PAGE_SIZE), F32),
                        pltpu.VMEM((nc, bpc, 128), F32)])
    return pl.pallas_call(
        functools.partial(_moba_keys_kernel, layer=layer, depth=depth, pc=pc, past_len=past_len),
        grid_spec=grid_spec,
        out_shape=[jax.ShapeDtypeStruct((nb, nc, MB_HEADS, pc * PAGE_SIZE), F32),
                   jax.ShapeDtypeStruct((nb, MB_HEADS, 128), F32)],
        compiler_params=_cparams(("arbitrary", "arbitrary")),
        name="moba_keys",
    )(page_table, q_pad, k_new.reshape(nb, 1, 128), slopes_col, cache)


def _moba_values_kernel(pt_ref, p_ref, pown_ref, vnew_ref, cache_ref, o_ref, buf_ref, sem_ref, acc_ref,
                        *, layer, depth, pc):
    c = pl.program_id(1)
    nc = pl.num_programs(1)
    slot = _stream_step(pt_ref, cache_ref, buf_ref, sem_ref, layer=layer, depth=depth, pc=pc)

    @pl.when(c == 0)
    def _():
        acc_ref[...] = jnp.zeros_like(acc_ref)

    vals = buf_ref[slot].reshape(pc * PAGE_SIZE, 128)
    acc = acc_ref[...] + _dot(p_ref[0, 0].astype(BF16), vals.astype(BF16))
    acc_ref[...] = acc

    @pl.when(c == nc - 1)
    def _():
        o = acc + pown_ref[0].astype(BF16).astype(F32) * vnew_ref[0].astype(BF16).astype(F32)
        row = lax.broadcasted_iota(jnp.int32, (MB_HEADS, 64), 0)
        o_ref[0] = jnp.where(row < MB_GROUP, o[:, 0:64], o[:, 64:128])


def _moba_values(page_table, p, p_own, v_new, cache, layer, depth):
    nb, n_pages = page_table.shape
    nc, pcl = p.shape[1], p.shape[3]
    pc = pcl // PAGE_SIZE
    grid_spec = pltpu.PrefetchScalarGridSpec(
        num_scalar_prefetch=1,
        grid=(nb, nc),
        in_specs=[pl.BlockSpec((1, 1, MB_HEADS, pcl), lambda b, c, pt: (b, c, 0, 0)),
                  pl.BlockSpec((1, MB_HEADS, 128), lambda b, c, pt: (b, 0, 0)),
                  pl.BlockSpec((1, 1, 128), lambda b, c, pt: (b, 0, 0)),
                  pl.BlockSpec(memory_space=pl.ANY)],
        out_specs=pl.BlockSpec((1, MB_HEADS, 64), lambda b, c, pt: (b, 0, 0)),
        scratch_shapes=[pltpu.VMEM((2, pc, PAGE_SIZE, 128), F32),
                        pltpu.SemaphoreType.DMA((2,)),
                        pltpu.VMEM((MB_HEADS, 128), F32)])
    return pl.pallas_call(
        functools.partial(_moba_values_kernel, layer=layer, depth=depth, pc=pc),
        grid_spec=grid_spec,
        out_shape=jax.ShapeDtypeStruct((nb, MB_HEADS, 64), F32),
        compiler_params=_cparams(("arbitrary", "arbitrary")),
        name="moba_values",
    )(page_table, p, p_own, v_new.reshape(nb, 1, 128), cache)


def _merge_kernel(ya_ref, yb_ref, yc_ref, gate_ref, mg_ref, x_ref, wb_ref, wo_ref, g_ref, b_ref, o_ref, *, alpha):
    merged = None
    for n, y_ref in enumerate((ya_ref, yb_ref, yc_ref)):
        gcol = gate_ref[:, n * BR_WIDTH:(n + 1) * BR_WIDTH]
        o = y_ref[...] * (gcol * _sigmoid(gcol))
        br = _dot(o.astype(BF16), wb_ref[n])
        term = br * _sigmoid(mg_ref[:, n * D_MODEL:(n + 1) * D_MODEL])
        merged = term if merged is None else merged + term
    z = alpha * x_ref[...] + _dot(merged.astype(BF16), wo_ref[...])
    mu = jnp.mean(z, axis=-1, keepdims=True)
    zc = z - mu
    var = jnp.mean(zc * zc, axis=-1, keepdims=True)
    o_ref[...] = zc * lax.rsqrt(var + 1e-5) * g_ref[...] + b_ref[...]


def _merge(ya, yb, yc, proj, x, wb, wo, ln_g, ln_b, alpha):
    m = x.shape[0]
    tm = _pick(m, (256, 128))
    row = lambda w, j=0: pl.BlockSpec((tm, w), lambda i: (i, j))
    return pl.pallas_call(
        functools.partial(_merge_kernel, alpha=alpha),
        grid=(m // tm,),
        in_specs=[row(512), row(512), row(512),
                  row(1536, OFF_GATE // 1536), row(3072, OFF_MERGE // 3072), row(D_MODEL),
                  pl.BlockSpec((N_BRANCH, BR_WIDTH, D_MODEL), lambda i: (0, 0, 0)),
                  pl.BlockSpec((D_MODEL, D_MODEL), lambda i: (0, 0)),
                  pl.BlockSpec((1, D_MODEL), lambda i: (0, 0)),
                  pl.BlockSpec((1, D_MODEL), lambda i: (0, 0))],
        out_specs=row(D_MODEL),
        out_shape=jax.ShapeDtypeStruct((m, D_MODEL), F32),
        compiler_params=_cparams(("parallel",)),
        name="merge",
    )(ya, yb, yc, proj, proj, x, wb, wo, ln_g, ln_b)


def _rope_tables(pos):
    half = MLA_ROPE // 2
    inv = ROPE_THETA ** (-jnp.arange(half, dtype=F32) / half)
    ang = pos.astype(F32)[:, None] * inv[None, :]
    return jnp.tile(jnp.cos(ang), (1, MLA_HEADS)), jnp.tile(jnp.sin(ang), (1, MLA_HEADS))


def _permute_w_in(w):
    d = w.shape[0]
    z = lambda n: jnp.zeros((d, n), w.dtype)
    return jnp.concatenate([w[:, 0:1664], z(128), w[:, 2592:2848], w[:, 1664:2080], z(96),
                            w[:, 2080:2592], w[:, 4384:7456], w[:, 2848:4384]], axis=1).astype(BF16)


def kernel(x_prompt, x_sample, cache_mla, cache_moba_k, cache_moba_v, state_wkv, state_shift, page_table, w_in, rw_mu, rw_w0, rw_w2, rw_a0, rw_a2, rw_k_k, rw_k_a, rw_r_k, rw_gn_g, rw_gn_b, mla_q_norm, mla_w_uq, mla_kv_norm, mla_w_uk, mla_w_uv, w_branch, w_out, ln_g, ln_b):
    bp, tp, d = x_prompt.shape
    bs, ts, _ = x_sample.shape
    assert ts == 1 and d == D_MODEL
    depth = w_in.shape[0]
    n_pool = cache_mla.shape[0]
    n_pages = page_table.shape[1]
    past_len = n_pages * PAGE_SIZE
    assert past_len % MB_BLOCK == 0
    alpha = (2 * depth) ** 0.25

    cache_mla2 = cache_mla.reshape(n_pool * depth, PAGE_SIZE, MLA_ROW)
    cache_k2 = cache_moba_k.reshape(n_pool * depth, PAGE_SIZE, MB_KV_HEADS * MB_HEAD_DIM)
    cache_v2 = cache_moba_v.reshape(n_pool * depth, PAGE_SIZE, MB_KV_HEADS * MB_HEAD_DIM)

    cos_p, sin_p = _rope_tables(jnp.tile(jnp.arange(tp), bp))
    cos_s, sin_s = _rope_tables(jnp.full((bs,), past_len))
    slopes = 2.0 ** (-8.0 * jnp.arange(1, MB_HEADS + 1, dtype=F32) / MB_HEADS)
    slopes_row = jnp.repeat(slopes.reshape(MB_KV_HEADS, MB_GROUP), MB_BLOCK, axis=1)
    slopes_col = slopes.reshape(MB_HEADS, 1)
    half = MLA_ROPE // 2
    hd = MLA_NOPE + MLA_ROPE

    xp = x_prompt.reshape(bp * tp, d)
    xs = x_sample.reshape(bs, d)
    zero_shift = jnp.zeros((bp, RW_SHIFT_W), F32)
    zero_wkv = jnp.zeros((bp, RW_HEADS, 64, 64), F32)

    mla_p, mla_s, kp, ksm, vp, vsm, wp, wsm, sp, ssm = [], [], [], [], [], [], [], [], [], []
    for l in range(depth):
        w_perm = _permute_w_in(w_in[l])
        r2 = lambda a: a.reshape(1, -1)
        rw = (r2(rw_mu[l]), r2(rw_w0[l]), rw_w2[l].astype(BF16), r2(rw_a0[l]), rw_a2[l].astype(BF16),
              r2(rw_k_k[l]), r2(rw_k_a[l]), r2(rw_r_k[l]), r2(rw_gn_g[l]), r2(rw_gn_b[l]))
        wuq3 = mla_w_uq[l].reshape(MLA_Q_RANK, MLA_HEADS, hd)
        wuq = jnp.concatenate([wuq3[:, :, :MLA_NOPE].reshape(MLA_Q_RANK, -1),
                               wuq3[:, :, MLA_NOPE:MLA_NOPE + half].reshape(MLA_Q_RANK, -1),
                               wuq3[:, :, MLA_NOPE + half:].reshape(MLA_Q_RANK, -1)], axis=1).astype(BF16)
        wukt = jnp.transpose(mla_w_uk[l], (1, 2, 0)).astype(BF16)
        wuv = jnp.transpose(mla_w_uv[l], (1, 0, 2)).astype(BF16)
        mw = (r2(mla_q_norm[l]), wuq, r2(mla_kv_norm[l]), wukt)
        wb = w_branch[l].astype(BF16)
        wo = w_out[l].astype(BF16)
        lg, lb = r2(ln_g[l]), r2(ln_b[l])

        proj = _inproj(xp, w_perm)
        ya, wkv_new = _rwkv_prompt(proj, zero_shift, zero_wkv, rw, bp, tp)
        rows, rowsb, qf = _mla_prep(proj, cos_p, sin_p, mw)
        yb = _mla_attn_prompt(qf, rowsb, wuv, bp, tp)
        yc = _moba_prompt(proj, slopes_row, bp, tp)
        xp_new = _merge(ya, yb, yc, proj, xp, wb, wo, lg, lb, alpha)
        mla_p.append(rows.reshape(bp, tp, MLA_ROW))
        kp.append(proj[:, OFF_KV:OFF_KV + 128].reshape(bp, tp, MB_KV_HEADS, MB_HEAD_DIM))
        vp.append(proj[:, OFF_KV + 128:OFF_KV + 256].reshape(bp, tp, MB_KV_HEADS, MB_HEAD_DIM))
        wp.append(wkv_new)
        sp.append(proj.reshape(bp, tp, PROJ_W)[:, tp - 1, :RW_SHIFT_W])
        xp = xp_new

        proj = _inproj(xs, w_perm)
        ya, wkv_new = _rwkv_sample(proj, state_shift[l], state_wkv[l], rw)
        rows, rowsb, qf = _mla_prep(proj, cos_s, sin_s, mw)
        o_lat = _mla_decode(page_table, jnp.transpose(qf, (1, 0, 2)), rowsb, cache_mla2, l, depth)
        yb = _mla_out(jnp.transpose(o_lat, (1, 0, 2)), wuv)
        k_new = proj[:, OFF_KV:OFF_KV + 128]
        v_new = proj[:, OFF_KV + 128:OFF_KV + 256]
        q4 = proj[:, OFF_MQ:OFF_MQ + 512].reshape(bs, MB_KV_HEADS, MB_GROUP, MB_HEAD_DIM)
        q_pad = jnp.concatenate(
            [jnp.pad(q4[:, g], ((0, 0), (0, 0), (g * 64, (MB_KV_HEADS - 1 - g) * 64))) for g in range(MB_KV_HEADS)],
            axis=1)
        p_att, p_own = _moba_keys(page_table, q_pad, k_new, slopes_col, cache_k2, l, depth)
        yc = _moba_values(page_table, p_att, p_own, v_new, cache_v2, l, depth).reshape(bs, MB_HEADS * 64)
        xs_new = _merge(ya, yb, yc, proj, xs, wb, wo, lg, lb, alpha)
        mla_s.append(rows.reshape(bs, 1, MLA_ROW))
        ksm.append(k_new.reshape(bs, 1, MB_KV_HEADS, MB_HEAD_DIM))
        vsm.append(v_new.reshape(bs, 1, MB_KV_HEADS, MB_HEAD_DIM))
        wsm.append(wkv_new)
        ssm.append(proj[:, :RW_SHIFT_W])
        xs = xs_new

    return (xp.reshape(bp, tp, d), xs.reshape(bs, 1, d),
            jnp.stack(mla_p, 1), jnp.stack(mla_s, 1),
            jnp.stack(kp, 1), jnp.stack(ksm, 1), jnp.stack(vp, 1), jnp.stack(vsm, 1),
            jnp.stack(wp, 0), jnp.stack(wsm, 0), jnp.stack(sp, 0), jnp.stack(ssm, 0))
```

```python
import functools
import math

import jax
import jax.numpy as jnp
from jax import lax
from jax.experimental import pallas as pl
from jax.experimental.pallas import tpu as pltpu

F32 = jnp.float32
BF16 = jnp.bfloat16
HIGHEST = lax.Precision.HIGHEST

D_MODEL = 1024
PAGE_SIZE = 128
RW_HEADS = 8
RW_HEAD_DIM = 64
RW_WIDTH = 512
RW_DECAY_RANK = 64
RW_SHIFT_W = 1664
RW_GN_EPS = 64e-5
MLA_HEADS = 8
MLA_NOPE = 64
MLA_ROPE = 32
MLA_V = 64
MLA_Q_RANK = 256
MLA_KV_RANK = 128
MLA_ROW = 160
ROPE_THETA = 10000.0
MB_HEADS = 8
MB_KV_HEADS = 2
MB_HEAD_DIM = 64
MB_GROUP = 4
MB_BLOCK = 256
MB_TOPK = 3
N_BRANCH = 3
BR_WIDTH = 512
NEG_BIG = -1e30

OFF_RW = 0
OFF_KV = 1792
OFF_MLA = 2048
OFF_MQ = 2560
OFF_MERGE = 3072
OFF_GATE = 6144
PROJ_W = 7680

VMEM_LIMIT = 56 * 1024 * 1024


def _cparams(sem, vmem=VMEM_LIMIT):
    return pltpu.CompilerParams(dimension_semantics=sem, vmem_limit_bytes=vmem)


def _pick(n, prefs):
    for p in prefs:
        if n % p == 0:
            return p
    return n


def _dot(a, b, **kw):
    return jnp.dot(a, b, preferred_element_type=F32, **kw)


def _dot_nt(a, b, **kw):
    return lax.dot_general(a, b, (((1,), (1,)), ((), ())), preferred_element_type=F32, **kw)


def _split_bf16(x):
    hi = x.astype(BF16)
    lo = (x - hi.astype(F32)).astype(BF16)
    return hi, lo


def _mm_any(f, a, b, passes):
    if passes == 1:
        return f(a.astype(BF16), b.astype(BF16))
    a_hi, a_lo = _split_bf16(a)
    b_hi, b_lo = _split_bf16(b)
    return f(a_hi, b_hi) + f(a_hi, b_lo) + f(a_lo, b_hi)


def _mm(a, b, passes):
    return _mm_any(_dot, a, b, passes)


def _mm_nt(a, b, passes):
    return _mm_any(_dot_nt, a, b, passes)


def _bmm(a, b, passes):
    return _mm_any(lambda x, y: jnp.einsum('hij,hjk->hik', x, y, preferred_element_type=F32), a, b, passes)


def _bmm_nt(a, b, passes):
    return _mm_any(lambda x, y: jnp.einsum('hik,hjk->hij', x, y, preferred_element_type=F32), a, b, passes)


RW_PASSES_A = 1
RW_PASSES_INV = 3


def _sigmoid(x):
    return 1.0 / (1.0 + jnp.exp(-x))


def _inproj_kernel(x_ref, w_ref, o_ref, xb_ref):
    @pl.when(pl.program_id(1) == 0)
    def _():
        xb_ref[...] = x_ref[...].astype(BF16)

    o_ref[...] = _dot(xb_ref[...], w_ref[...])


def _inproj(x, w_perm):
    m, d = x.shape
    tm = _pick(m, (512, 256, 128))
    tn = 1280
    return pl.pallas_call(
        _inproj_kernel,
        grid=(m // tm, PROJ_W // tn),
        in_specs=[pl.BlockSpec((tm, d), lambda i, j: (i, 0)),
                  pl.BlockSpec((d, tn), lambda i, j: (0, j))],
        out_specs=pl.BlockSpec((tm, tn), lambda i, j: (i, j)),
        out_shape=jax.ShapeDtypeStruct((m, PROJ_W), F32),
        scratch_shapes=[pltpu.VMEM((tm, d), BF16)],
        compiler_params=_cparams(("parallel", "arbitrary")),
        name="inproj",
    )(x, w_perm)


def _rwkv_prep(cols, prev, mu, w0, w2b, a0, a2b, k_k, k_a):
    mixed = cols + (prev - cols) * mu
    r = mixed[:, 0:512]
    k = mixed[:, 512:1024]
    v = mixed[:, 1024:1536]
    wd = mixed[:, 1536:1600]
    ad = mixed[:, 1600:1664]
    zw = w0 + _dot(jnp.tanh(wd).astype(BF16), w2b)
    nz = -zw
    softplus = jnp.maximum(nz, 0.0) + jnp.log(1.0 + jnp.exp(-jnp.abs(nz)))
    lw = -jnp.exp(-softplus - 0.5)
    a = _sigmoid(a0 + _dot(ad.astype(BF16), a2b))
    kk = k * k_k
    k_mod = k * (1.0 + (a - 1.0) * k_a)
    return r, k_mod, v, lw, a, kk


def _head_norm(kk_h):
    ss = jnp.sum(kk_h * kk_h, axis=-1, keepdims=True)
    return kk_h * lax.rsqrt(jnp.maximum(ss, 1e-24))


def _group_norm_bonus(y_h, r_h, kmod_h, v_h, rk_h, g_h, b_h):
    ym = jnp.mean(y_h, axis=-1, keepdims=True)
    yc = y_h - ym
    yv = jnp.mean(yc * yc, axis=-1, keepdims=True)
    yn = yc * lax.rsqrt(yv + RW_GN_EPS) * g_h + b_h
    bonus = jnp.sum(r_h * kmod_h * rk_h, axis=-1, keepdims=True) * v_h
    return yn + bonus


def _rwkv_chunk_kernel(p_ref, shift0_ref, wkv0_ref, mu_ref, w0_ref, w2_ref, a0_ref, a2_ref,
                       kk_ref, ka_ref, rk_ref, gg_ref, gb_ref,
                       y_ref, wkv_ref, carry_ref, s_ref, *, chunk):
    c = pl.program_id(1)
    nc = pl.num_programs(1)
    C = chunk

    @pl.when(c == 0)
    def _():
        carry_ref[...] = shift0_ref[0]
        s_ref[...] = wkv0_ref[0]

    cols = p_ref[...]
    rows = lax.broadcasted_iota(jnp.int32, cols.shape, 0)
    prev = jnp.where(rows == 0, carry_ref[...], pltpu.roll(cols, 1, 0))
    carry_ref[...] = cols[C - 1:C, :]

    r, k_mod, v, lw, a, kk = _rwkv_prep(cols, prev, mu_ref[...], w0_ref[...], w2_ref[...],
                                        a0_ref[...], a2_ref[...], kk_ref[...], ka_ref[...])

    ti = lax.broadcasted_iota(jnp.int32, (C, C), 0)
    si = lax.broadcasted_iota(jnp.int32, (C, C), 1)
    lower = ti >= si
    strict = ti > si
    eye = (ti == si).astype(F32)
    tri = lower.astype(BF16)
    lw_hi = lw.astype(BF16)
    lw_r = lw - lw_hi.astype(F32)
    lw_mid = lw_r.astype(BF16)
    lw_lo = (lw_r - lw_mid.astype(F32)).astype(BF16)
    cum = _dot(tri, lw_hi) + _dot(tri, lw_mid) + _dot(tri, lw_lo)
    p_incl = jnp.exp(cum)
    p_excl = jnp.exp(cum - lw)
    p_inv = jnp.exp(-cum)
    p_end = p_incl[C - 1:C, :]

    base = 16
    diag_mask = (ti // base) == (si // base)
    off_masks = []
    size = base
    while size < C:
        off_masks.append(((ti // (2 * size)) == (si // (2 * size))) & ((ti // size) != (si // size)))
        size *= 2
    heads = lambda x: jnp.stack([x[:, h * 64:(h + 1) * 64] for h in range(RW_HEADS)], axis=0)
    kk3 = heads(kk)
    kk3 = kk3 * lax.rsqrt(jnp.maximum(jnp.sum(kk3 * kk3, axis=-1, keepdims=True), 1e-24))
    v3, r3, km3 = heads(v), heads(r), heads(k_mod)
    pinv3 = heads(p_inv)
    at = -(kk3 * heads(p_excl))
    bt = kk3 * heads(a) * pinv3
    kt = km3 * pinv3
    rt = r3 * heads(p_incl)
    s0 = s_ref[...]

    lhs = jnp.concatenate([at, rt], axis=1)
    rhs = jnp.concatenate([bt, kt], axis=1)
    mx = _bmm_nt(lhs, rhs, RW_PASSES_A)
    a_ab = jnp.where(strict[None], mx[:, :C, :C], 0.0)
    a_ak = jnp.where(strict[None], mx[:, :C, C:], 0.0)
    r_b = jnp.where(lower[None], mx[:, C:, :C], 0.0)
    r_k = jnp.where(lower[None], mx[:, C:, C:], 0.0)

    npow = jnp.where(diag_mask[None], a_ab, 0.0)
    inv = eye[None] + npow
    for _ in range(3):
        npow = _bmm(npow, npow, RW_PASSES_INV)
        inv = inv + _bmm(npow, inv, RW_PASSES_INV)
    for off_mask in off_masks:
        inv = inv + _bmm(inv, _bmm(jnp.where(off_mask[None], a_ab, 0.0), inv, RW_PASSES_INV), RW_PASSES_INV)

    sa = _bmm_nt(lhs, s0, RW_PASSES_A)
    rhs_u = sa[:, :C] + _bmm(a_ak, v3, RW_PASSES_A)
    u = _bmm(inv, rhs_u, RW_PASSES_INV)
    y3 = sa[:, C:] + _bmm(r_b, u, RW_PASSES_A) + _bmm(r_k, v3, RW_PASSES_A)

    for h in range(RW_HEADS):
        sl = slice(h * 64, (h + 1) * 64)
        pe = p_end[:, sl]
        uv_t = jnp.concatenate([u[h], v3[h]], axis=1).T
        bk = jnp.concatenate([bt[h] * pe, kt[h] * pe], axis=1)
        upd = _mm(uv_t, bk, RW_PASSES_A)
        s_ref[h] = s0[h] * pe + upd[:64, :64] + upd[64:, 64:]
        y_ref[:, sl] = _group_norm_bonus(y3[h], r3[h], km3[h], v3[h], rk_ref[:, sl], gg_ref[:, sl], gb_ref[:, sl])

    @pl.when(c == nc - 1)
    def _():
        wkv_ref[0] = s_ref[...]


def _rwkv_prompt(proj, shift0, wkv0, rw, nb, t):
    chunk = 128
    assert t % chunk == 0
    nc = t // chunk
    vec = lambda n: pl.BlockSpec((1, n), lambda b, c: (0, 0))
    mat = lambda a, b_: pl.BlockSpec((a, b_), lambda b, c: (0, 0))
    return pl.pallas_call(
        functools.partial(_rwkv_chunk_kernel, chunk=chunk),
        grid=(nb, nc),
        in_specs=[pl.BlockSpec((chunk, RW_SHIFT_W), lambda b, c: (b * nc + c, 0)),
                  pl.BlockSpec((1, 1, RW_SHIFT_W), lambda b, c: (b, 0, 0)),
                  pl.BlockSpec((1, RW_HEADS, 64, 64), lambda b, c: (b, 0, 0, 0)),
                  vec(RW_SHIFT_W), vec(512), mat(64, 512), vec(512), mat(64, 512),
                  vec(512), vec(512), vec(512), vec(512), vec(512)],
        out_specs=[pl.BlockSpec((chunk, 512), lambda b, c: (b * nc + c, 0)),
                   pl.BlockSpec((1, RW_HEADS, 64, 64), lambda b, c: (b, 0, 0, 0))],
        out_shape=[jax.ShapeDtypeStruct((nb * t, 512), F32),
                   jax.ShapeDtypeStruct((nb, RW_HEADS, 64, 64), F32)],
        scratch_shapes=[pltpu.VMEM((1, RW_SHIFT_W), F32), pltpu.VMEM((RW_HEADS, 64, 64), F32)],
        compiler_params=_cparams(("parallel", "arbitrary")),
        name="rwkv_prompt",
    )(proj, shift0.reshape(nb, 1, RW_SHIFT_W), wkv0, *rw)


def _rwkv_step_kernel(p_ref, shift0_ref, wkv0_ref, mu_ref, w0_ref, w2_ref, a0_ref, a2_ref,
                      kk_ref, ka_ref, rk_ref, gg_ref, gb_ref,
                      y_ref, wkv_ref, r_s, k_s, v_s, wt_s, kt_s, kkt_s, bt_s, rt_s, vt_s, yt_s):
    h = pl.program_id(0)
    nh = pl.num_programs(0)

    @pl.when(h == 0)
    def _():
        r, k_mod, v, lw, a, kk = _rwkv_prep(p_ref[:, :RW_SHIFT_W], shift0_ref[...], mu_ref[...], w0_ref[...],
                                            w2_ref[...], a0_ref[...], a2_ref[...], kk_ref[...], ka_ref[...])
        kkn = jnp.concatenate([_head_norm(kk[:, g * 64:(g + 1) * 64]) for g in range(RW_HEADS)], axis=1)
        r_s[...] = r
        k_s[...] = k_mod
        v_s[...] = v
        wt_s[...] = jnp.exp(lw).T
        kt_s[...] = k_mod.T
        kkt_s[...] = kkn.T
        bt_s[...] = (kkn * a).T
        rt_s[...] = r.T
        vt_s[...] = v.T

    base = pl.multiple_of(h * 64, 64)
    w_t = wt_s[pl.ds(base, 64), :]
    k_t = kt_s[pl.ds(base, 64), :]
    kk_t = kkt_s[pl.ds(base, 64), :]
    b_t = bt_s[pl.ds(base, 64), :]
    r_t = rt_s[pl.ds(base, 64), :]

    def body(vi, carry):
        s0 = wkv0_ref[0, vi]
        sa = -jnp.sum(s0 * kk_t, axis=0, keepdims=True)
        s1 = s0 * w_t + sa * b_t + vt_s[pl.ds(base + vi, 1), :] * k_t
        wkv_ref[0, vi] = s1
        yt_s[pl.ds(base + vi, 1), :] = jnp.sum(s1 * r_t, axis=0, keepdims=True)
        return carry

    lax.fori_loop(0, RW_HEAD_DIM, body, 0)

    @pl.when(h == nh - 1)
    def _():
        y = yt_s[...].T
        for g in range(RW_HEADS):
            sl = slice(g * 64, (g + 1) * 64)
            y_ref[:, sl] = _group_norm_bonus(y[:, sl], r_s[:, sl], k_s[:, sl], v_s[:, sl],
                                             rk_ref[:, sl], gg_ref[:, sl], gb_ref[:, sl])


def _rwkv_sample(proj, shift0, wkv0_t, rw):
    nb = proj.shape[0]
    vec = lambda n: pl.BlockSpec((1, n), lambda i: (0, 0))
    mat = lambda a, b_: pl.BlockSpec((a, b_), lambda i: (0, 0))
    sq = pltpu.VMEM((nb, 512), F32)
    st = pltpu.VMEM((512, nb), F32)
    return pl.pallas_call(
        _rwkv_step_kernel,
        grid=(RW_HEADS,),
        in_specs=[pl.BlockSpec((nb, 1792), lambda i: (0, 0)),
                  mat(nb, RW_SHIFT_W),
                  pl.BlockSpec((1, 64, 64, nb), lambda i: (i, 0, 0, 0)),
                  vec(RW_SHIFT_W), vec(512), mat(64, 512), vec(512), mat(64, 512),
                  vec(512), vec(512), vec(512), vec(512), vec(512)],
        out_specs=[pl.BlockSpec((nb, 512), lambda i: (0, 0)),
                   pl.BlockSpec((1, 64, 64, nb), lambda i: (i, 0, 0, 0))],
        out_shape=[jax.ShapeDtypeStruct((nb, 512), F32),
                   jax.ShapeDtypeStruct((RW_HEADS, 64, 64, nb), F32)],
        scratch_shapes=[sq, sq, sq, st, st, st, st, st, st, st],
        compiler_params=_cparams(("arbitrary",)),
        name="rwkv_sample",
    )(proj, shift0, wkv0_t, *rw)


def _rms(x, g):
    return x * lax.rsqrt(jnp.mean(x * x, axis=-1, keepdims=True) + 1e-6) * g


def _mla_prep_kernel(p_ref, cos_ref, sin_ref, qn_ref, wuq_ref, kvn_ref, wukt_ref,
                     rows_ref, rowsb_ref, ct_ref, qf_ref):
    blk = p_ref[...]
    tm = blk.shape[0]
    qd = blk[:, 0:256]
    kvd = blk[:, 256:384]
    kr = blk[:, 384:416]
    q = _dot(_rms(qd, qn_ref[...]).astype(BF16), wuq_ref[...])
    cos = cos_ref[...]
    sin = sin_ref[...]
    x1 = q[:, 512:640]
    x2 = q[:, 640:768]
    r1 = x1 * cos - x2 * sin
    r2 = x2 * cos + x1 * sin
    c_kv = _rms(kvd, kvn_ref[...])
    c16 = cos[:, :16]
    s16 = sin[:, :16]
    k1 = kr[:, :16]
    k2 = kr[:, 16:32]
    k_rope = jnp.concatenate([k1 * c16 - k2 * s16, k2 * c16 + k1 * s16], axis=-1)
    rows_ref[...] = jnp.concatenate([c_kv, k_rope], axis=-1)
    pad = jnp.zeros((tm, 96), F32)
    rowsb_ref[...] = jnp.concatenate([c_kv, k_rope, pad], axis=-1).astype(BF16)
    ct_ref[...] = c_kv.T.astype(BF16)
    scale = (MLA_NOPE + MLA_ROPE) ** -0.5
    for h in range(MLA_HEADS):
        ql = _dot(q[:, h * 64:(h + 1) * 64].astype(BF16), wukt_ref[h])
        qr = jnp.concatenate([r1[:, h * 16:(h + 1) * 16], r2[:, h * 16:(h + 1) * 16]], axis=-1)
        qf_ref[h] = (jnp.concatenate([ql, qr, pad], axis=-1) * scale).astype(BF16)


def _mla_prep(proj, cos, sin, mw):
    m = proj.shape[0]
    tm = _pick(m, (256, 128))
    q_norm, wuq, kv_norm, wukt = mw
    return pl.pallas_call(
        _mla_prep_kernel,
        grid=(m // tm,),
        in_specs=[pl.BlockSpec((tm, 512), lambda i: (i, OFF_MLA // 512)),
                  pl.BlockSpec((tm, 128), lambda i: (i, 0)),
                  pl.BlockSpec((tm, 128), lambda i: (i, 0)),
                  pl.BlockSpec((1, 256), lambda i: (0, 0)),
                  pl.BlockSpec((256, 768), lambda i: (0, 0)),
                  pl.BlockSpec((1, 128), lambda i: (0, 0)),
                  pl.BlockSpec((MLA_HEADS, 64, 128), lambda i: (0, 0, 0))],
        out_specs=[pl.BlockSpec((tm, MLA_ROW), lambda i: (i, 0)),
                   pl.BlockSpec((tm, 256), lambda i: (i, 0)),
                   pl.BlockSpec((MLA_KV_RANK, tm), lambda i: (0, i)),
                   pl.BlockSpec((MLA_HEADS, tm, 256), lambda i: (0, i, 0))],
        out_shape=[jax.ShapeDtypeStruct((m, MLA_ROW), F32),
                   jax.ShapeDtypeStruct((m, 256), BF16),
                   jax.ShapeDtypeStruct((MLA_KV_RANK, m), BF16),
                   jax.ShapeDtypeStruct((MLA_HEADS, m, 256), BF16)],
        compiler_params=_cparams(("parallel",)),
        name="mla_prep",
    )(proj, cos, sin, q_norm, wuq, kv_norm, wukt)


def _mla_attn_kernel(q_ref, k_ref, ct_ref, wuvt_ref, y_ref, m_ref, l_ref, acc_ref, *, tq):
    i = pl.program_id(1)
    j = pl.program_id(2)
    nh = MLA_HEADS

    @pl.when(j == 0)
    def _():
        m_ref[...] = jnp.full_like(m_ref, NEG_BIG)
        l_ref[...] = jnp.zeros_like(l_ref)
        acc_ref[...] = jnp.zeros_like(acc_ref)

    @pl.when(j <= i)
    def _():
        q = q_ref[...].reshape(nh * tq, 256)
        s = _dot_nt(k_ref[...], q)
        key = lax.broadcasted_iota(jnp.int32, s.shape, 0)
        qry = lax.broadcasted_iota(jnp.int32, s.shape, 1) % tq
        s = jnp.where(key <= qry + (i - j) * tq, s, NEG_BIG)
        m_old = m_ref[...]
        m_new = jnp.maximum(m_old, jnp.max(s, axis=0, keepdims=True))
        alpha = jnp.exp(m_old - m_new)
        p = jnp.exp(s - m_new)
        l_ref[...] = alpha * l_ref[...] + jnp.sum(p, axis=0, keepdims=True)
        acc_ref[...] = alpha * acc_ref[...] + _dot(ct_ref[...], p.astype(BF16))
        m_ref[...] = m_new

    @pl.when(j == i)
    def _():
        o_t = (acc_ref[...] / l_ref[...]).astype(BF16)
        y_t = jnp.concatenate([_dot(wuvt_ref[h], o_t[:, h * tq:(h + 1) * tq]) for h in range(nh)], axis=0)
        y_ref[...] = y_t.T


def _mla_attn_prompt(qf, rowsb, ct, wuvt, nb, t):
    tq = _pick(t, (256, 128))
    nq = t // tq
    return pl.pallas_call(
        functools.partial(_mla_attn_kernel, tq=tq),
        grid=(nb, nq, nq),
        in_specs=[pl.BlockSpec((MLA_HEADS, tq, 256), lambda b, i, j: (0, b * nq + i, 0)),
                  pl.BlockSpec((tq, 256), lambda b, i, j: (b * nq + jnp.minimum(i, j), 0)),
                  pl.BlockSpec((MLA_KV_RANK, tq), lambda b, i, j: (0, b * nq + jnp.minimum(i, j))),
                  pl.BlockSpec((MLA_HEADS, 64, 128), lambda b, i, j: (0, 0, 0))],
        out_specs=pl.BlockSpec((tq, 512), lambda b, i, j: (b * nq + i, 0)),
        out_shape=jax.ShapeDtypeStruct((nb * t, 512), F32),
        scratch_shapes=[pltpu.VMEM((1, MLA_HEADS * tq), F32), pltpu.VMEM((1, MLA_HEADS * tq), F32),
                        pltpu.VMEM((MLA_KV_RANK, MLA_HEADS * tq), F32)],
        compiler_params=_cparams(("parallel", "parallel", "arbitrary")),
        name="mla_attn_prompt",
    )(qf, rowsb, ct, wuvt)


def _page_copy(cache_ref, buf_ref, sem_ref, row, slot, j):
    return pltpu.make_async_copy(cache_ref.at[row], buf_ref.at[slot, j], sem_ref.at[slot])


def _stream_step(pt_ref, cache_ref, buf_ref, sem_ref, *, layer, depth, pc):
    b = pl.program_id(0)
    c = pl.program_id(1)
    nb = pl.num_programs(0)
    nc = pl.num_programs(1)
    step = b * nc + c
    slot = step % 2

    def fetch(bb, cc, sl):
        for j in range(pc):
            row = pt_ref[bb, cc * pc + j] * depth + layer
            _page_copy(cache_ref, buf_ref, sem_ref, row, sl, j).start()

    @pl.when(step == 0)
    def _():
        fetch(b, c, slot)

    @pl.when(step + 1 < nb * nc)
    def _():
        wrap = c + 1 == nc
        fetch(jnp.where(wrap, b + 1, b), jnp.where(wrap, 0, c + 1), 1 - slot)

    for j in range(pc):
        _page_copy(cache_ref, buf_ref, sem_ref, 0, slot, j).wait()
    return slot


def _mla_decode_kernel(pt_ref, q_ref, new_ref, cache_ref, o_ref, buf_ref, sem_ref, m_ref, l_ref, acc_ref,
                       *, layer, depth, pc):
    c = pl.program_id(1)
    nc = pl.num_programs(1)
    slot = _stream_step(pt_ref, cache_ref, buf_ref, sem_ref, layer=layer, depth=depth, pc=pc)

    @pl.when(c == 0)
    def _():
        m_ref[...] = jnp.full_like(m_ref, NEG_BIG)
        l_ref[...] = jnp.zeros_like(l_ref)
        acc_ref[...] = jnp.zeros_like(acc_ref)

    q = q_ref[0]
    ql = jnp.broadcast_to(q[None, :, :MLA_KV_RANK], (pc, MLA_HEADS, MLA_KV_RANK))
    qr = jnp.broadcast_to(q[None, :, MLA_KV_RANK:MLA_ROW], (pc, MLA_HEADS, MLA_ROPE))
    kt = buf_ref[slot]
    kc = kt[:, :MLA_KV_RANK, :].astype(BF16)
    kr = kt[:, MLA_KV_RANK:, :].astype(BF16)
    s = (jnp.einsum('phk,pkn->phn', ql, kc, preferred_element_type=F32)
         + jnp.einsum('phk,pkn->phn', qr, kr, preferred_element_type=F32))
    m_old = m_ref[...]
    m_new = jnp.maximum(m_old, jnp.max(jnp.max(s, axis=0), axis=-1, keepdims=True))
    alpha = jnp.exp(m_old - m_new)
    p = jnp.exp(s - m_new[None])
    l_new = alpha * l_ref[...] + jnp.sum(jnp.sum(p, axis=0), axis=-1, keepdims=True)
    pv = jnp.einsum('phn,pcn->phc', p.astype(BF16), kc, preferred_element_type=F32)
    acc_new = alpha * acc_ref[...] + jnp.sum(pv, axis=0)
    m_ref[...] = m_new
    l_ref[...] = l_new
    acc_ref[...] = acc_new

    @pl.when(c == nc - 1)
    def _():
        new = new_ref[0].astype(F32)
        s_n = jnp.sum(q.astype(F32) * new, axis=-1, keepdims=True)
        m_f = jnp.maximum(m_new, s_n)
        al = jnp.exp(m_new - m_f)
        p_n = jnp.exp(s_n - m_f)
        l_f = al * l_new + p_n
        acc_f = al * acc_new + p_n.astype(BF16).astype(F32) * new[:, :MLA_KV_RANK]
        o_ref[0] = acc_f / l_f


def _mla_decode(page_table, qf_s, rowsb_s, cache, layer, depth):
    nb, n_pages = page_table.shape
    pc = _pick(n_pages, (128, 64, 32, 16, 8, 4, 2, 1))
    nc = n_pages // pc
    grid_spec = pltpu.PrefetchScalarGridSpec(
        num_scalar_prefetch=1,
        grid=(nb, nc),
        in_specs=[pl.BlockSpec((1, MLA_HEADS, 256), lambda b, c, pt: (b, 0, 0)),
                  pl.BlockSpec((1, 1, 256), lambda b, c, pt: (b, 0, 0)),
                  pl.BlockSpec(memory_space=pl.ANY)],
        out_specs=pl.BlockSpec((1, MLA_HEADS, MLA_KV_RANK), lambda b, c, pt: (b, 0, 0)),
        scratch_shapes=[pltpu.VMEM((2, pc, MLA_ROW, PAGE_SIZE), F32),
                        pltpu.SemaphoreType.DMA((2,)),
                        pltpu.VMEM((MLA_HEADS, 1), F32), pltpu.VMEM((MLA_HEADS, 1), F32),
                        pltpu.VMEM((MLA_HEADS, MLA_KV_RANK), F32)])
    return pl.pallas_call(
        functools.partial(_mla_decode_kernel, layer=layer, depth=depth, pc=pc),
        grid_spec=grid_spec,
        out_shape=jax.ShapeDtypeStruct((nb, MLA_HEADS, MLA_KV_RANK), F32),
        compiler_params=_cparams(("arbitrary", "arbitrary")),
        name="mla_decode",
    )(page_table, qf_s, rowsb_s.reshape(nb, 1, 256), cache)


def _mla_out_kernel(o_ref, wuv_ref, y_ref):
    for h in range(MLA_HEADS):
        y_ref[:, h * 64:(h + 1) * 64] = _dot(o_ref[h].astype(BF16), wuv_ref[h])


def _mla_out(o_hm, wuv):
    m = o_hm.shape[1]
    return pl.pallas_call(
        _mla_out_kernel,
        grid=(1,),
        in_specs=[pl.BlockSpec((MLA_HEADS, m, 128), lambda i: (0, 0, 0)),
                  pl.BlockSpec((MLA_HEADS, 128, 64), lambda i: (0, 0, 0))],
        out_specs=pl.BlockSpec((m, 512), lambda i: (0, 0)),
        out_shape=jax.ShapeDtypeStruct((m, 512), F32),
        compiler_params=_cparams(("arbitrary",)),
        name="mla_out",
    )(o_hm, wuv)


def _topk_select(gate, gate_row, n_valid, topk):
    nblk = gate.shape[0]
    blk = lax.broadcasted_iota(jnp.int32, gate.shape, 0)
    rank = jnp.zeros(gate.shape, F32)
    for m in range(nblk):
        g_m = gate_row(m)
        ahead = jnp.where(g_m > gate, 1.0, jnp.where((g_m == gate) & (m < blk), 1.0, 0.0))
        rank = rank + ahead * jnp.where(m < n_valid, 1.0, 0.0)
    return jnp.where((rank < topk) & (blk < n_valid), 1.0, 0.0)


def _moba_prompt_kernel(q_ref, kv_ref, slope_ref, y_ref, vt_ref, km_ref, gate_ref, sel_ref, *, nblk):
    i = pl.program_id(1)
    nq = MB_GROUP * MB_BLOCK
    scale = MB_HEAD_DIM ** -0.5

    @pl.when(i == 0)
    def _():
        for n in range(nblk):
            blk = kv_ref[n * MB_BLOCK:(n + 1) * MB_BLOCK, :]
            vt_ref[n] = blk[:, 128:256].T
            km_ref[n:n + 1, :] = jnp.mean(blk[:, 0:128], axis=0, keepdims=True)

    q = q_ref[...]
    lane = lax.broadcasted_iota(jnp.int32, (1, nq), 1)
    qpos = (i * MB_BLOCK + lane % MB_BLOCK).astype(F32)
    koff = lax.broadcasted_iota(jnp.int32, (MB_BLOCK, 1), 0).astype(F32)
    blk_id = lax.broadcasted_iota(jnp.int32, (nblk, nq), 0)
    outs = []
    for g in range(MB_KV_HEADS):
        qs = jnp.concatenate([q[:, (g * MB_GROUP + e) * 64:(g * MB_GROUP + e + 1) * 64]
                              for e in range(MB_GROUP)], axis=0)
        qsb = (qs * scale).astype(BF16)
        slope = slope_ref[g:g + 1, :]
        gate = _dot_nt(km_ref[:, g * 64:(g + 1) * 64], qs, precision=HIGHEST)
        gate_ref[...] = gate
        sel_ref[...] = _topk_select(gate, lambda m: gate_ref[m:m + 1, :], i, MB_TOPK)

        def scores(n):
            kb = kv_ref[pl.ds(pl.multiple_of(n * MB_BLOCK, MB_BLOCK), MB_BLOCK), g * 64:(g + 1) * 64]
            s = _dot_nt(kb.astype(BF16), qsb)
            kpos = (n * MB_BLOCK).astype(F32) + koff
            return s - slope * (qpos - kpos), kpos

        def accumulate(n, s, carry):
            m_old, l_old, acc = carry
            m_new = jnp.maximum(m_old, jnp.max(s, axis=0, keepdims=True))
            alpha = jnp.exp(m_old - m_new)
            p = jnp.exp(s - m_new)
            l_new = alpha * l_old + jnp.sum(p, axis=0, keepdims=True)
            vt = vt_ref[n][g * 64:(g + 1) * 64, :]
            acc = alpha * acc + _dot(vt.astype(BF16), p.astype(BF16))
            return m_new, l_new, acc

        s_own, kpos = scores(i)
        s_own = jnp.where(kpos <= qpos, s_own, NEG_BIG)
        init = (jnp.full((1, nq), NEG_BIG, F32), jnp.zeros((1, nq), F32), jnp.zeros((64, nq), F32))
        carry = accumulate(i, s_own, init)

        def body(n, carry):
            s, _ = scores(n)
            s = jnp.where(sel_ref[pl.ds(n, 1), :] > 0.5, s, NEG_BIG)
            return accumulate(n, s, carry)

        _, l_f, acc = lax.fori_loop(0, i, body, carry)
        outs.append(acc / l_f)

    ot = jnp.concatenate(outs, axis=0).T
    for g in range(MB_KV_HEADS):
        for e in range(MB_GROUP):
            hh = g * MB_GROUP + e
            y_ref[:, hh * 64:(hh + 1) * 64] = ot[e * MB_BLOCK:(e + 1) * MB_BLOCK, g * 64:(g + 1) * 64]


def _moba_prompt(proj, slopes_row, nb, t):
    assert t % MB_BLOCK == 0
    nblk = t // MB_BLOCK
    return pl.pallas_call(
        functools.partial(_moba_prompt_kernel, nblk=nblk),
        grid=(nb, nblk),
        in_specs=[pl.BlockSpec((MB_BLOCK, 512), lambda b, i: (b * nblk + i, OFF_MQ // 512)),
                  pl.BlockSpec((t, 256), lambda b, i: (b, OFF_KV // 256)),
                  pl.BlockSpec((MB_KV_HEADS, MB_GROUP * MB_BLOCK), lambda b, i: (0, 0))],
        out_specs=pl.BlockSpec((MB_BLOCK, 512), lambda b, i: (b * nblk + i, 0)),
        out_shape=jax.ShapeDtypeStruct((nb * t, 512), F32),
        scratch_shapes=[pltpu.VMEM((nblk, 128, MB_BLOCK), F32), pltpu.VMEM((nblk, 128), F32),
                        pltpu.VMEM((nblk, MB_GROUP * MB_BLOCK), F32), pltpu.VMEM((nblk, MB_GROUP * MB_BLOCK), F32)],
        compiler_params=_cparams(("parallel", "arbitrary")),
        name="moba_prompt",
    )(proj, proj, slopes_row)


def _moba_keys_kernel(pt_ref, q_ref, knew_ref, slope_ref, cache_ref, p_ref, pown_ref, idx_ref,
                      buf_ref, sem_ref, *, layer, depth, pc, past_len):
    slot = _stream_step(pt_ref, cache_ref, buf_ref, sem_ref, layer=layer, depth=depth, pc=pc)
    ppb = MB_BLOCK // PAGE_SIZE
    nblk = pc // ppb
    scale = MB_HEAD_DIM ** -0.5
    q = q_ref[0]
    qb = (q * scale).astype(BF16)
    rep = lambda x, n: jnp.broadcast_to(x[None], (n,) + x.shape)
    bmm = lambda a, b: jnp.einsum('phf,pfn->phn', a, b, preferred_element_type=F32)
    kt = buf_ref[slot]
    s = bmm(rep(qb, pc), kt.astype(BF16))

    kt4 = kt.reshape(nblk, ppb, 128, PAGE_SIZE)
    ks = kt4[:, 0]
    for e in range(1, ppb):
        ks = ks + kt4[:, e]
    q_hi, q_lo = _split_bf16(q)
    ks_hi, ks_lo = _split_bf16(ks)
    g3 = bmm(rep(q_hi, nblk), ks_hi) + bmm(rep(q_hi, nblk), ks_lo) + bmm(rep(q_lo, nblk), ks_hi)
    gate = jnp.sum(g3, axis=-1, keepdims=True)

    blk_f = lax.broadcasted_iota(jnp.int32, gate.shape, 0).astype(F32)
    sel = jnp.zeros(gate.shape, F32)
    picks = []
    for _ in range(MB_TOPK):
        top = jnp.max(gate, axis=0, keepdims=True)
        pick = jnp.min(jnp.where(gate == top, blk_f, float(nblk)), axis=0, keepdims=True)
        hit = blk_f == pick
        sel = jnp.where(hit, 1.0, sel)
        gate = jnp.where(hit, NEG_BIG, gate)
        picks.append(pick)
    idx_ref[0] = jnp.concatenate(picks, axis=0).astype(jnp.int32)

    shp = (nblk, ppb, MB_HEADS, PAGE_SIZE)
    pos = (lax.broadcasted_iota(jnp.int32, shp, 0) * MB_BLOCK + lax.broadcasted_iota(jnp.int32, shp, 1) * PAGE_SIZE
           + lax.broadcasted_iota(jnp.int32, shp, 3))
    dist = (past_len - pos).astype(F32)
    slope = slope_ref[...]
    s_m = jnp.where(sel[:, None] > 0.5, s.reshape(shp) - slope[None, None] * dist, NEG_BIG)
    s_own = jnp.sum(qb.astype(F32) * knew_ref[0].astype(BF16).astype(F32), axis=-1, keepdims=True)
    m = jnp.maximum(jnp.max(jnp.max(s_m, axis=(0, 1)), axis=-1, keepdims=True), s_own)
    e = jnp.exp(s_m - m[None, None])
    e_own = jnp.exp(s_own - m)
    inv = 1.0 / (jnp.sum(jnp.sum(e, axis=(0, 1)), axis=-1, keepdims=True) + e_own)
    p_ref[0] = (e * inv[None, None]).reshape(pc, MB_HEADS, PAGE_SIZE)
    pown_ref[0] = jnp.broadcast_to(e_own * inv, (MB_HEADS, 128))


def _moba_keys(page_table, q_pad, k_new, slopes_col, cache, layer, depth):
    nb, n_pages = page_table.shape
    pc = n_pages
    past_len = n_pages * PAGE_SIZE
    assert past_len // MB_BLOCK >= MB_TOPK
    grid_spec = pltpu.PrefetchScalarGridSpec(
        num_scalar_prefetch=1,
        grid=(nb, 1),
        in_specs=[pl.BlockSpec((1, MB_HEADS, 128), lambda b, c, pt: (b, 0, 0)),
                  pl.BlockSpec((1, 1, 128), lambda b, c, pt: (b, 0, 0)),
                  pl.BlockSpec((MB_HEADS, 1), lambda b, c, pt: (0, 0)),
                  pl.BlockSpec(memory_space=pl.ANY)],
        out_specs=[pl.BlockSpec((1, pc, MB_HEADS, PAGE_SIZE), lambda b, c, pt: (b, 0, 0, 0)),
                   pl.BlockSpec((1, MB_HEADS, 128), lambda b, c, pt: (b, 0, 0)),
                   pl.BlockSpec((1, MB_TOPK, MB_HEADS, 1), lambda b, c, pt: (b, 0, 0, 0))],
        scratch_shapes=[pltpu.VMEM((2, pc, 128, PAGE_SIZE), F32),
                        pltpu.SemaphoreType.DMA((2,))])
    return pl.pallas_call(
        functools.partial(_moba_keys_kernel, layer=layer, depth=depth, pc=pc, past_len=past_len),
        grid_spec=grid_spec,
        out_shape=[jax.ShapeDtypeStruct((nb, pc, MB_HEADS, PAGE_SIZE), F32),
                   jax.ShapeDtypeStruct((nb, MB_HEADS, 128), F32),
                   jax.ShapeDtypeStruct((nb, MB_TOPK, MB_HEADS, 1), jnp.int32)],
        compiler_params=_cparams(("arbitrary", "arbitrary")),
        name="moba_keys",
    )(page_table, q_pad, k_new.reshape(nb, 1, 128), slopes_col, cache)


def _moba_values_kernel(pt_ref, idx_ref, p_ref, pown_ref, vnew_ref, cache_ref, o_ref, buf_ref, sem_ref,
                        *, layer, depth):
    b = pl.program_id(0)
    nb = pl.num_programs(0)
    slot = b % 2
    ppb = MB_BLOCK // PAGE_SIZE
    n_half = MB_HEADS * MB_TOPK * ppb

    def half_page(row, h, sl, i):
        return pltpu.make_async_copy(cache_ref.at[row, pl.ds((h // MB_GROUP) * 64, 64), :],
                                     buf_ref.at[sl, i], sem_ref.at[sl])

    def fetch(bb, sl):
        for h in range(MB_HEADS):
            for j in range(MB_TOPK):
                blk = idx_ref[bb, h * MB_TOPK + j]
                for e in range(ppb):
                    row = pt_ref[bb, blk * ppb + e] * depth + layer
                    half_page(row, h, sl, (h * MB_TOPK + j) * ppb + e).start()

    @pl.when(b == 0)
    def _():
        fetch(b, slot)

    @pl.when(b + 1 < nb)
    def _():
        fetch(b + 1, 1 - slot)

    for i in range(n_half):
        half_page(0, 0, slot, i).wait()

    row_id = lax.broadcasted_iota(jnp.int32, (MB_HEADS, 1), 0)
    acc = jnp.zeros((MB_HEADS, 64), F32)
    for h in range(MB_HEADS):
        vts, prs = [], []
        for j in range(MB_TOPK):
            blk = idx_ref[b, h * MB_TOPK + j]
            for e in range(ppb):
                vts.append(buf_ref[slot, (h * MB_TOPK + j) * ppb + e])
                prs.append(p_ref[0, blk * ppb + e, h:h + 1, :])
        vt = jnp.concatenate(vts, axis=1).astype(BF16)
        pr = jnp.concatenate(prs, axis=1)
        lhs = jnp.where(row_id == h, pr, 0.0).astype(BF16)
        acc = acc + _dot_nt(lhs, vt)
    v_new = vnew_ref[0].astype(BF16).astype(F32)
    v8 = jnp.where(row_id < MB_GROUP, v_new[:, 0:64], v_new[:, 64:128])
    o_ref[0] = acc + pown_ref[0][:, 0:64].astype(BF16).astype(F32) * v8


def _moba_values(page_table, idx, p, p_own, v_new, cache, layer, depth):
    nb, n_pages = page_table.shape
    ppb = MB_BLOCK // PAGE_SIZE
    n_half = MB_HEADS * MB_TOPK * ppb
    grid_spec = pltpu.PrefetchScalarGridSpec(
        num_scalar_prefetch=2,
        grid=(nb,),
        in_specs=[pl.BlockSpec((1, n_pages, MB_HEADS, PAGE_SIZE), lambda b, pt, ix: (b, 0, 0, 0)),
                  pl.BlockSpec((1, MB_HEADS, 128), lambda b, pt, ix: (b, 0, 0)),
                  pl.BlockSpec((1, 1, 128), lambda b, pt, ix: (b, 0, 0)),
                  pl.BlockSpec(memory_space=pl.ANY)],
        out_specs=pl.BlockSpec((1, MB_HEADS, 64), lambda b, pt, ix: (b, 0, 0)),
        scratch_shapes=[pltpu.VMEM((2, n_half, 64, PAGE_SIZE), F32),
                        pltpu.SemaphoreType.DMA((2,))])
    return pl.pallas_call(
        functools.partial(_moba_values_kernel, layer=layer, depth=depth),
        grid_spec=grid_spec,
        out_shape=jax.ShapeDtypeStruct((nb, MB_HEADS, 64), F32),
        compiler_params=_cparams(("arbitrary",)),
        name="moba_values",
    )(page_table, idx, p, p_own, v_new.reshape(nb, 1, 128), cache)


def _merge_kernel(ya_ref, yb_ref, yc_ref, gate_ref, mg_ref, x_ref, wb_ref, wo_ref, g_ref, b_ref, o_ref, *, alpha):
    merged = None
    for n, y_ref in enumerate((ya_ref, yb_ref, yc_ref)):
        gcol = gate_ref[:, n * BR_WIDTH:(n + 1) * BR_WIDTH]
        o = y_ref[...] * (gcol * _sigmoid(gcol))
        br = _dot(o.astype(BF16), wb_ref[n])
        term = br * _sigmoid(mg_ref[:, n * D_MODEL:(n + 1) * D_MODEL])
        merged = term if merged is None else merged + term
    z = alpha * x_ref[...] + _dot(merged.astype(BF16), wo_ref[...])
    mu = jnp.mean(z, axis=-1, keepdims=True)
    zc = z - mu
    var = jnp.mean(zc * zc, axis=-1, keepdims=True)
    o_ref[...] = zc * lax.rsqrt(var + 1e-5) * g_ref[...] + b_ref[...]


def _merge(ya, yb, yc, proj, x, wb, wo, ln_g, ln_b, alpha):
    m = x.shape[0]
    tm = _pick(m, (256, 128))
    row = lambda w, j=0: pl.BlockSpec((tm, w), lambda i: (i, j))
    return pl.pallas_call(
        functools.partial(_merge_kernel, alpha=alpha),
        grid=(m // tm,),
        in_specs=[row(512), row(512), row(512),
                  row(1536, OFF_GATE // 1536), row(3072, OFF_MERGE // 3072), row(D_MODEL),
                  pl.BlockSpec((N_BRANCH, BR_WIDTH, D_MODEL), lambda i: (0, 0, 0)),
                  pl.BlockSpec((D_MODEL, D_MODEL), lambda i: (0, 0)),
                  pl.BlockSpec((1, D_MODEL), lambda i: (0, 0)),
                  pl.BlockSpec((1, D_MODEL), lambda i: (0, 0))],
        out_specs=row(D_MODEL),
        out_shape=jax.ShapeDtypeStruct((m, D_MODEL), F32),
        compiler_params=_cparams(("parallel",)),
        name="merge",
    )(ya, yb, yc, proj, proj, x, wb, wo, ln_g, ln_b)


def _rope_tables(pos):
    half = MLA_ROPE // 2
    inv = ROPE_THETA ** (-jnp.arange(half, dtype=F32) / half)
    ang = pos.astype(F32)[:, None] * inv[None, :]
    return jnp.tile(jnp.cos(ang), (1, MLA_HEADS)), jnp.tile(jnp.sin(ang), (1, MLA_HEADS))


def _permute_w_in(w):
    d = w.shape[0]
    z = lambda n: jnp.zeros((d, n), w.dtype)
    return jnp.concatenate([w[:, 0:1664], z(128), w[:, 2592:2848], w[:, 1664:2080], z(96),
                            w[:, 2080:2592], w[:, 4384:7456], w[:, 2848:4384]], axis=1).astype(BF16)


def kernel(x_prompt, x_sample, cache_mla, cache_moba_k, cache_moba_v, state_wkv, state_shift, page_table, w_in, rw_mu, rw_w0, rw_w2, rw_a0, rw_a2, rw_k_k, rw_k_a, rw_r_k, rw_gn_g, rw_gn_b, mla_q_norm, mla_w_uq, mla_kv_norm, mla_w_uk, mla_w_uv, w_branch, w_out, ln_g, ln_b):
    bp, tp, d = x_prompt.shape
    bs, ts, _ = x_sample.shape
    assert ts == 1 and d == D_MODEL
    depth = w_in.shape[0]
    n_pool = cache_mla.shape[0]
    n_pages = page_table.shape[1]
    past_len = n_pages * PAGE_SIZE
    assert past_len % MB_BLOCK == 0
    alpha = (2 * depth) ** 0.25

    cache_mla2 = jnp.transpose(cache_mla, (0, 1, 3, 2)).reshape(n_pool * depth, MLA_ROW, PAGE_SIZE)
    cache_k2 = jnp.transpose(cache_moba_k, (0, 1, 3, 4, 2)).reshape(n_pool * depth, 128, PAGE_SIZE)
    cache_v2 = jnp.transpose(cache_moba_v, (0, 1, 3, 4, 2)).reshape(n_pool * depth, 128, PAGE_SIZE)

    cos_p, sin_p = _rope_tables(jnp.tile(jnp.arange(tp), bp))
    cos_s, sin_s = _rope_tables(jnp.full((bs,), past_len))
    slopes = 2.0 ** (-8.0 * jnp.arange(1, MB_HEADS + 1, dtype=F32) / MB_HEADS)
    slopes_row = jnp.repeat(slopes.reshape(MB_KV_HEADS, MB_GROUP), MB_BLOCK, axis=1)
    slopes_col = slopes.reshape(MB_HEADS, 1)
    half = MLA_ROPE // 2
    hd = MLA_NOPE + MLA_ROPE

    xp = x_prompt.reshape(bp * tp, d)
    xs = x_sample.reshape(bs, d)
    zero_shift = jnp.zeros((bp, RW_SHIFT_W), F32)
    zero_wkv = jnp.zeros((bp, RW_HEADS, 64, 64), F32)

    mla_p, mla_s, kp, ksm, vp, vsm, wp, wsm, sp, ssm = [], [], [], [], [], [], [], [], [], []
    for l in range(depth):
        w_perm = _permute_w_in(w_in[l])
        r2 = lambda a: a.reshape(1, -1)
        rw = (r2(rw_mu[l]), r2(rw_w0[l]), rw_w2[l].astype(BF16), r2(rw_a0[l]), rw_a2[l].astype(BF16),
              r2(rw_k_k[l]), r2(rw_k_a[l]), r2(rw_r_k[l]), r2(rw_gn_g[l]), r2(rw_gn_b[l]))
        wuq3 = mla_w_uq[l].reshape(MLA_Q_RANK, MLA_HEADS, hd)
        wuq = jnp.concatenate([wuq3[:, :, :MLA_NOPE].reshape(MLA_Q_RANK, -1),
                               wuq3[:, :, MLA_NOPE:MLA_NOPE + half].reshape(MLA_Q_RANK, -1),
                               wuq3[:, :, MLA_NOPE + half:].reshape(MLA_Q_RANK, -1)], axis=1).astype(BF16)
        wukt = jnp.transpose(mla_w_uk[l], (1, 2, 0)).astype(BF16)
        wuv = jnp.transpose(mla_w_uv[l], (1, 0, 2)).astype(BF16)
        wuvt = jnp.transpose(mla_w_uv[l], (1, 2, 0)).astype(BF16)
        mw = (r2(mla_q_norm[l]), wuq, r2(mla_kv_norm[l]), wukt)
        wb = w_branch[l].astype(BF16)
        wo = w_out[l].astype(BF16)
        lg, lb = r2(ln_g[l]), r2(ln_b[l])

        proj = _inproj(xp, w_perm)
        ya, wkv_new = _rwkv_prompt(proj, zero_shift, zero_wkv, rw, bp, tp)
        rows, rowsb, ct, qf = _mla_prep(proj, cos_p, sin_p, mw)
        yb = _mla_attn_prompt(qf, rowsb, ct, wuvt, bp, tp)
        yc = _moba_prompt(proj, slopes_row, bp, tp)
        xp_new = _merge(ya, yb, yc, proj, xp, wb, wo, lg, lb, alpha)
        mla_p.append(rows.reshape(bp, tp, MLA_ROW))
        kp.append(proj[:, OFF_KV:OFF_KV + 128].reshape(bp, tp, MB_KV_HEADS, MB_HEAD_DIM))
        vp.append(proj[:, OFF_KV + 128:OFF_KV + 256].reshape(bp, tp, MB_KV_HEADS, MB_HEAD_DIM))
        wp.append(wkv_new)
        sp.append(proj.reshape(bp, tp, PROJ_W)[:, tp - 1, :RW_SHIFT_W])
        xp = xp_new

        proj = _inproj(xs, w_perm)
        ya, wkv_t = _rwkv_sample(proj, state_shift[l], jnp.transpose(state_wkv[l], (1, 2, 3, 0)), rw)
        wkv_new = jnp.transpose(wkv_t, (3, 0, 1, 2))
        rows, rowsb, _, qf = _mla_prep(proj, cos_s, sin_s, mw)
        o_lat = _mla_decode(page_table, jnp.transpose(qf, (1, 0, 2)), rowsb, cache_mla2, l, depth)
        yb = _mla_out(jnp.transpose(o_lat, (1, 0, 2)), wuv)
        k_new = proj[:, OFF_KV:OFF_KV + 128]
        v_new = proj[:, OFF_KV + 128:OFF_KV + 256]
        q4 = proj[:, OFF_MQ:OFF_MQ + 512].reshape(bs, MB_KV_HEADS, MB_GROUP, MB_HEAD_DIM)
        q_pad = jnp.concatenate(
            [jnp.pad(q4[:, g], ((0, 0), (0, 0), (g * 64, (MB_KV_HEADS - 1 - g) * 64))) for g in range(MB_KV_HEADS)],
            axis=1)
        p_att, p_own, idx = _moba_keys(page_table, q_pad, k_new, slopes_col, cache_k2, l, depth)
        idx = jnp.transpose(idx[..., 0], (0, 2, 1)).reshape(bs, MB_HEADS * MB_TOPK)
        yc = _moba_values(page_table, idx, p_att, p_own, v_new, cache_v2, l, depth).reshape(bs, MB_HEADS * 64)
        xs_new = _merge(ya, yb, yc, proj, xs, wb, wo, lg, lb, alpha)
        mla_s.append(rows.reshape(bs, 1, MLA_ROW))
        ksm.append(k_new.reshape(bs, 1, MB_KV_HEADS, MB_HEAD_DIM))
        vsm.append(v_new.reshape(bs, 1, MB_KV_HEADS, MB_HEAD_DIM))
        wsm.append(wkv_new)
        ssm.append(proj[:, :RW_SHIFT_W])
        xs = xs_new

    return (xp.reshape(bp, tp, d), xs.reshape(bs, 1, d),
            jnp.stack(mla_p, 1), jnp.stack(mla_s, 1),
            jnp.stack(kp, 1), jnp.stack(ksm, 1), jnp.stack(vp, 1), jnp.stack(vsm, 1),
            jnp.stack(wp, 0), jnp.stack(wsm, 0), jnp.stack(sp, 0), jnp.stack(ssm, 0))
```

```python
import functools
import math

import jax
import jax.numpy as jnp
from jax import lax
from jax.experimental import pallas as pl
from jax.experimental.pallas import tpu as pltpu

F32 = jnp.float32
BF16 = jnp.bfloat16
HIGHEST = lax.Precision.HIGHEST

D_MODEL = 1024
PAGE_SIZE = 128
RW_HEADS = 8
RW_HEAD_DIM = 64
RW_WIDTH = 512
RW_DECAY_RANK = 64
RW_SHIFT_W = 1664
RW_GN_EPS = 64e-5
MLA_HEADS = 8
MLA_NOPE = 64
MLA_ROPE = 32
MLA_V = 64
MLA_Q_RANK = 256
MLA_KV_RANK = 128
MLA_ROW = 160
ROPE_THETA = 10000.0
MB_HEADS = 8
MB_KV_HEADS = 2
MB_HEAD_DIM = 64
MB_GROUP = 4
MB_BLOCK = 256
MB_TOPK = 3
N_BRANCH = 3
BR_WIDTH = 512
NEG_BIG = -1e30

OFF_RW = 0
OFF_KV = 1792
OFF_MLA = 2048
OFF_MQ = 2560
OFF_MERGE = 3072
OFF_GATE = 6144
PROJ_W = 7680

VMEM_LIMIT = 56 * 1024 * 1024


def _cparams(sem, vmem=VMEM_LIMIT):
    return pltpu.CompilerParams(dimension_semantics=sem, vmem_limit_bytes=vmem)


def _pick(n, prefs):
    for p in prefs:
        if n % p == 0:
            return p
    return n


def _dot(a, b, **kw):
    return jnp.dot(a, b, preferred_element_type=F32, **kw)


def _dot_nt(a, b, **kw):
    return lax.dot_general(a, b, (((1,), (1,)), ((), ())), preferred_element_type=F32, **kw)


def _split_bf16(x):
    hi = x.astype(BF16)
    lo = (x - hi.astype(F32)).astype(BF16)
    return hi, lo


def _mm_any(f, a, b, passes):
    if passes == 1:
        return f(a.astype(BF16), b.astype(BF16))
    a_hi, a_lo = _split_bf16(a)
    b_hi, b_lo = _split_bf16(b)
    return f(a_hi, b_hi) + f(a_hi, b_lo) + f(a_lo, b_hi)


def _mm(a, b, passes):
    return _mm_any(_dot, a, b, passes)


def _mm_nt(a, b, passes):
    return _mm_any(_dot_nt, a, b, passes)


def _bmm(a, b, passes):
    return _mm_any(lambda x, y: jnp.einsum('hij,hjk->hik', x, y, preferred_element_type=F32), a, b, passes)


def _bmm_nt(a, b, passes):
    return _mm_any(lambda x, y: jnp.einsum('hik,hjk->hij', x, y, preferred_element_type=F32), a, b, passes)


RW_PASSES_A = 1
RW_PASSES_INV = 1


def _sigmoid(x):
    return 1.0 / (1.0 + jnp.exp(-x))


def _inproj_kernel(x_ref, w_ref, o_ref, xb_ref):
    @pl.when(pl.program_id(1) == 0)
    def _():
        xb_ref[...] = x_ref[...].astype(BF16)

    o_ref[...] = _dot(xb_ref[...], w_ref[...])


def _inproj(x, w_perm):
    m, d = x.shape
    tm = _pick(m, (1024, 512, 256, 128))
    tn = 1280
    return pl.pallas_call(
        _inproj_kernel,
        grid=(m // tm, PROJ_W // tn),
        in_specs=[pl.BlockSpec((tm, d), lambda i, j: (i, 0)),
                  pl.BlockSpec((d, tn), lambda i, j: (0, j))],
        out_specs=pl.BlockSpec((tm, tn), lambda i, j: (i, j)),
        out_shape=jax.ShapeDtypeStruct((m, PROJ_W), F32),
        scratch_shapes=[pltpu.VMEM((tm, d), BF16)],
        compiler_params=_cparams(("parallel", "arbitrary")),
        name="inproj",
    )(x, w_perm)


def _rwkv_prep(cols, prev, mu, w0, w2b, a0, a2b, k_k, k_a):
    mixed = cols + (prev - cols) * mu
    r = mixed[:, 0:512]
    k = mixed[:, 512:1024]
    v = mixed[:, 1024:1536]
    wd = mixed[:, 1536:1600]
    ad = mixed[:, 1600:1664]
    zw = w0 + _dot(jnp.tanh(wd).astype(BF16), w2b)
    nz = -zw
    softplus = jnp.maximum(nz, 0.0) + jnp.log(1.0 + jnp.exp(-jnp.abs(nz)))
    lw = -jnp.exp(-softplus - 0.5)
    a = _sigmoid(a0 + _dot(ad.astype(BF16), a2b))
    kk = k * k_k
    k_mod = k * (1.0 + (a - 1.0) * k_a)
    return r, k_mod, v, lw, a, kk


def _head_norm(kk_h):
    ss = jnp.sum(kk_h * kk_h, axis=-1, keepdims=True)
    return kk_h * lax.rsqrt(jnp.maximum(ss, 1e-24))


def _group_norm_bonus(y_h, r_h, kmod_h, v_h, rk_h, g_h, b_h):
    ym = jnp.mean(y_h, axis=-1, keepdims=True)
    yc = y_h - ym
    yv = jnp.mean(yc * yc, axis=-1, keepdims=True)
    yn = yc * lax.rsqrt(yv + RW_GN_EPS) * g_h + b_h
    bonus = jnp.sum(r_h * kmod_h * rk_h, axis=-1, keepdims=True) * v_h
    return yn + bonus


def _rwkv_chunk_kernel(p_ref, shift0_ref, wkv0_ref, mu_ref, w0_ref, w2_ref, a0_ref, a2_ref,
                       kk_ref, ka_ref, rk_ref, gg_ref, gb_ref,
                       y_ref, wkv_ref, carry_ref, s_ref, *, chunk):
    c = pl.program_id(1)
    nc = pl.num_programs(1)
    C = chunk

    @pl.when(c == 0)
    def _():
        carry_ref[...] = shift0_ref[0]
        s_ref[...] = wkv0_ref[0]

    cols = p_ref[...]
    rows = lax.broadcasted_iota(jnp.int32, cols.shape, 0)
    prev = jnp.where(rows == 0, carry_ref[...], pltpu.roll(cols, 1, 0))
    carry_ref[...] = cols[C - 1:C, :]

    r, k_mod, v, lw, a, kk = _rwkv_prep(cols, prev, mu_ref[...], w0_ref[...], w2_ref[...],
                                        a0_ref[...], a2_ref[...], kk_ref[...], ka_ref[...])

    ti = lax.broadcasted_iota(jnp.int32, (C, C), 0)
    si = lax.broadcasted_iota(jnp.int32, (C, C), 1)
    lower = ti >= si
    strict = ti > si
    eye = (ti == si).astype(F32)
    tri = lower.astype(BF16)
    lw_hi = lw.astype(BF16)
    lw_r = lw - lw_hi.astype(F32)
    lw_mid = lw_r.astype(BF16)
    lw_lo = (lw_r - lw_mid.astype(F32)).astype(BF16)
    cum = _dot(tri, lw_hi) + _dot(tri, lw_mid) + _dot(tri, lw_lo)
    p_incl = jnp.exp(cum)
    p_excl = jnp.exp(cum - lw)
    p_inv = jnp.exp(-cum)
    p_end = p_incl[C - 1:C, :]

    base = 16
    diag_mask = (ti // base) == (si // base)
    off_masks = []
    size = base
    while size < C:
        off_masks.append(((ti // (2 * size)) == (si // (2 * size))) & ((ti // size) != (si // size)))
        size *= 2
    heads = lambda x: jnp.stack([x[:, h * 64:(h + 1) * 64] for h in range(RW_HEADS)], axis=0)
    kk3 = heads(kk)
    kk3 = kk3 * lax.rsqrt(jnp.maximum(jnp.sum(kk3 * kk3, axis=-1, keepdims=True), 1e-24))
    v3, r3, km3 = heads(v), heads(r), heads(k_mod)
    pinv3 = heads(p_inv)
    at = -(kk3 * heads(p_excl))
    bt = kk3 * heads(a) * pinv3
    kt = km3 * pinv3
    rt = r3 * heads(p_incl)
    s0 = s_ref[...]

    lhs = jnp.concatenate([at, rt], axis=1)
    rhs = jnp.concatenate([bt, kt], axis=1)
    mx = _bmm_nt(lhs, rhs, RW_PASSES_A)
    a_ab = jnp.where(strict[None], mx[:, :C, :C], 0.0)
    a_ak = jnp.where(strict[None], mx[:, :C, C:], 0.0)
    r_b = jnp.where(lower[None], mx[:, C:, :C], 0.0)
    r_k = jnp.where(lower[None], mx[:, C:, C:], 0.0)

    npow = jnp.where(diag_mask[None], a_ab, 0.0)
    inv = eye[None] + npow
    for _ in range(3):
        npow = _bmm(npow, npow, RW_PASSES_INV)
        inv = inv + _bmm(npow, inv, RW_PASSES_INV)
    for off_mask in off_masks:
        inv = inv + _bmm(inv, _bmm(jnp.where(off_mask[None], a_ab, 0.0), inv, RW_PASSES_INV), RW_PASSES_INV)

    sa = _bmm_nt(lhs, s0, RW_PASSES_A)
    rhs_u = sa[:, :C] + _bmm(a_ak, v3, RW_PASSES_A)
    u = _bmm(inv, rhs_u, RW_PASSES_INV)
    y3 = sa[:, C:] + _bmm(r_b, u, RW_PASSES_A) + _bmm(r_k, v3, RW_PASSES_A)

    for h in range(RW_HEADS):
        sl = slice(h * 64, (h + 1) * 64)
        pe = p_end[:, sl]
        uv_t = jnp.concatenate([u[h], v3[h]], axis=1).T
        bk = jnp.concatenate([bt[h] * pe, kt[h] * pe], axis=1)
        upd = _mm(uv_t, bk, RW_PASSES_A)
        s_ref[h] = s0[h] * pe + upd[:64, :64] + upd[64:, 64:]
        y_ref[:, sl] = _group_norm_bonus(y3[h], r3[h], km3[h], v3[h], rk_ref[:, sl], gg_ref[:, sl], gb_ref[:, sl])

    @pl.when(c == nc - 1)
    def _():
        wkv_ref[0] = s_ref[...]


def _rwkv_prompt(proj, shift0, wkv0, rw, nb, t):
    chunk = 128
    assert t % chunk == 0
    nc = t // chunk
    vec = lambda n: pl.BlockSpec((1, n), lambda b, c: (0, 0))
    mat = lambda a, b_: pl.BlockSpec((a, b_), lambda b, c: (0, 0))
    return pl.pallas_call(
        functools.partial(_rwkv_chunk_kernel, chunk=chunk),
        grid=(nb, nc),
        in_specs=[pl.BlockSpec((chunk, RW_SHIFT_W), lambda b, c: (b * nc + c, 0)),
                  pl.BlockSpec((1, 1, RW_SHIFT_W), lambda b, c: (b, 0, 0)),
                  pl.BlockSpec((1, RW_HEADS, 64, 64), lambda b, c: (b, 0, 0, 0)),
                  vec(RW_SHIFT_W), vec(512), mat(64, 512), vec(512), mat(64, 512),
                  vec(512), vec(512), vec(512), vec(512), vec(512)],
        out_specs=[pl.BlockSpec((chunk, 512), lambda b, c: (b * nc + c, 0)),
                   pl.BlockSpec((1, RW_HEADS, 64, 64), lambda b, c: (b, 0, 0, 0))],
        out_shape=[jax.ShapeDtypeStruct((nb * t, 512), F32),
                   jax.ShapeDtypeStruct((nb, RW_HEADS, 64, 64), F32)],
        scratch_shapes=[pltpu.VMEM((1, RW_SHIFT_W), F32), pltpu.VMEM((RW_HEADS, 64, 64), F32)],
        compiler_params=_cparams(("parallel", "arbitrary")),
        name="rwkv_prompt",
    )(proj, shift0.reshape(nb, 1, RW_SHIFT_W), wkv0, *rw)


def _rwkv_step_kernel(p_ref, shift0_ref, wkv0_ref, mu_ref, w0_ref, w2_ref, a0_ref, a2_ref,
                      kk_ref, ka_ref, rk_ref, gg_ref, gb_ref,
                      y_ref, wkv_ref, r_s, k_s, v_s, wt_s, kt_s, kkt_s, bt_s, rt_s, vt_s, yt_s):
    h = pl.program_id(0)
    nh = pl.num_programs(0)

    @pl.when(h == 0)
    def _():
        r, k_mod, v, lw, a, kk = _rwkv_prep(p_ref[:, :RW_SHIFT_W], shift0_ref[...], mu_ref[...], w0_ref[...],
                                            w2_ref[...], a0_ref[...], a2_ref[...], kk_ref[...], ka_ref[...])
        kkn = jnp.concatenate([_head_norm(kk[:, g * 64:(g + 1) * 64]) for g in range(RW_HEADS)], axis=1)
        r_s[...] = r
        k_s[...] = k_mod
        v_s[...] = v
        wt_s[...] = jnp.exp(lw).T
        kt_s[...] = k_mod.T
        kkt_s[...] = kkn.T
        bt_s[...] = (kkn * a).T
        rt_s[...] = r.T
        vt_s[...] = v.T

    base = pl.multiple_of(h * 64, 64)
    w_t = wt_s[pl.ds(base, 64), :]
    k_t = kt_s[pl.ds(base, 64), :]
    kk_t = kkt_s[pl.ds(base, 64), :]
    b_t = bt_s[pl.ds(base, 64), :]
    r_t = rt_s[pl.ds(base, 64), :]

    def body(vi, carry):
        s0 = wkv0_ref[0, vi]
        sa = -jnp.sum(s0 * kk_t, axis=0, keepdims=True)
        s1 = s0 * w_t + sa * b_t + vt_s[pl.ds(base + vi, 1), :] * k_t
        wkv_ref[0, vi] = s1
        yt_s[pl.ds(base + vi, 1), :] = jnp.sum(s1 * r_t, axis=0, keepdims=True)
        return carry

    lax.fori_loop(0, RW_HEAD_DIM, body, 0)

    @pl.when(h == nh - 1)
    def _():
        y = yt_s[...].T
        for g in range(RW_HEADS):
            sl = slice(g * 64, (g + 1) * 64)
            y_ref[:, sl] = _group_norm_bonus(y[:, sl], r_s[:, sl], k_s[:, sl], v_s[:, sl],
                                             rk_ref[:, sl], gg_ref[:, sl], gb_ref[:, sl])


def _rwkv_sample(proj, shift0, wkv0_t, rw):
    nb = proj.shape[0]
    vec = lambda n: pl.BlockSpec((1, n), lambda i: (0, 0))
    mat = lambda a, b_: pl.BlockSpec((a, b_), lambda i: (0, 0))
    sq = pltpu.VMEM((nb, 512), F32)
    st = pltpu.VMEM((512, nb), F32)
    return pl.pallas_call(
        _rwkv_step_kernel,
        grid=(RW_HEADS,),
        in_specs=[pl.BlockSpec((nb, 1792), lambda i: (0, 0)),
                  mat(nb, RW_SHIFT_W),
                  pl.BlockSpec((1, 64, 64, nb), lambda i: (i, 0, 0, 0)),
                  vec(RW_SHIFT_W), vec(512), mat(64, 512), vec(512), mat(64, 512),
                  vec(512), vec(512), vec(512), vec(512), vec(512)],
        out_specs=[pl.BlockSpec((nb, 512), lambda i: (0, 0)),
                   pl.BlockSpec((1, 64, 64, nb), lambda i: (i, 0, 0, 0))],
        out_shape=[jax.ShapeDtypeStruct((nb, 512), F32),
                   jax.ShapeDtypeStruct((RW_HEADS, 64, 64, nb), F32)],
        scratch_shapes=[sq, sq, sq, st, st, st, st, st, st, st],
        compiler_params=_cparams(("arbitrary",)),
        name="rwkv_sample",
    )(proj, shift0, wkv0_t, *rw)


def _rms(x, g):
    return x * lax.rsqrt(jnp.mean(x * x, axis=-1, keepdims=True) + 1e-6) * g


def _mla_prep_kernel(p_ref, cos_ref, sin_ref, qn_ref, wuq_ref, kvn_ref, wukt_ref,
                     rows_ref, rowsb_ref, ct_ref, qf_ref):
    blk = p_ref[...]
    tm = blk.shape[0]
    qd = blk[:, 0:256]
    kvd = blk[:, 256:384]
    kr = blk[:, 384:416]
    q = _dot(_rms(qd, qn_ref[...]).astype(BF16), wuq_ref[...])
    cos = cos_ref[...]
    sin = sin_ref[...]
    x1 = q[:, 512:640]
    x2 = q[:, 640:768]
    r1 = x1 * cos - x2 * sin
    r2 = x2 * cos + x1 * sin
    c_kv = _rms(kvd, kvn_ref[...])
    c16 = cos[:, :16]
    s16 = sin[:, :16]
    k1 = kr[:, :16]
    k2 = kr[:, 16:32]
    k_rope = jnp.concatenate([k1 * c16 - k2 * s16, k2 * c16 + k1 * s16], axis=-1)
    rows_ref[...] = jnp.concatenate([c_kv, k_rope], axis=-1)
    pad = jnp.zeros((tm, 96), F32)
    rowsb_ref[...] = jnp.concatenate([c_kv, k_rope, pad], axis=-1).astype(BF16)
    ct_ref[...] = c_kv.T.astype(BF16)
    scale = (MLA_NOPE + MLA_ROPE) ** -0.5
    for h in range(MLA_HEADS):
        ql = _dot(q[:, h * 64:(h + 1) * 64].astype(BF16), wukt_ref[h])
        qr = jnp.concatenate([r1[:, h * 16:(h + 1) * 16], r2[:, h * 16:(h + 1) * 16]], axis=-1)
        qf_ref[h] = (jnp.concatenate([ql, qr, pad], axis=-1) * scale).astype(BF16)


def _mla_prep(proj, cos, sin, mw):
    m = proj.shape[0]
    tm = _pick(m, (256, 128))
    q_norm, wuq, kv_norm, wukt = mw
    return pl.pallas_call(
        _mla_prep_kernel,
        grid=(m // tm,),
        in_specs=[pl.BlockSpec((tm, 512), lambda i: (i, OFF_MLA // 512)),
                  pl.BlockSpec((tm, 128), lambda i: (i, 0)),
                  pl.BlockSpec((tm, 128), lambda i: (i, 0)),
                  pl.BlockSpec((1, 256), lambda i: (0, 0)),
                  pl.BlockSpec((256, 768), lambda i: (0, 0)),
                  pl.BlockSpec((1, 128), lambda i: (0, 0)),
                  pl.BlockSpec((MLA_HEADS, 64, 128), lambda i: (0, 0, 0))],
        out_specs=[pl.BlockSpec((tm, MLA_ROW), lambda i: (i, 0)),
                   pl.BlockSpec((tm, 256), lambda i: (i, 0)),
                   pl.BlockSpec((MLA_KV_RANK, tm), lambda i: (0, i)),
                   pl.BlockSpec((MLA_HEADS, tm, 256), lambda i: (0, i, 0))],
        out_shape=[jax.ShapeDtypeStruct((m, MLA_ROW), F32),
                   jax.ShapeDtypeStruct((m, 256), BF16),
                   jax.ShapeDtypeStruct((MLA_KV_RANK, m), BF16),
                   jax.ShapeDtypeStruct((MLA_HEADS, m, 256), BF16)],
        compiler_params=_cparams(("parallel",)),
        name="mla_prep",
    )(proj, cos, sin, q_norm, wuq, kv_norm, wukt)


def _mla_attn_kernel(it_ref, jt_ref, q_ref, k_ref, ct_ref, wuvt_ref, y_ref, m_ref, l_ref, acc_ref, *, tq):
    t = pl.program_id(1)
    i = it_ref[t]
    j = jt_ref[t]
    nh = MLA_HEADS

    @pl.when(j == 0)
    def _():
        m_ref[...] = jnp.full_like(m_ref, NEG_BIG)
        l_ref[...] = jnp.zeros_like(l_ref)
        acc_ref[...] = jnp.zeros_like(acc_ref)

    def step(causal):
        q = q_ref[...].reshape(nh * tq, 256)
        s = _dot_nt(k_ref[...], q)
        if causal:
            key = lax.broadcasted_iota(jnp.int32, s.shape, 0)
            qry = lax.broadcasted_iota(jnp.int32, s.shape, 1) % tq
            s = jnp.where(key <= qry, s, NEG_BIG)
        m_old = m_ref[...]
        m_new = jnp.maximum(m_old, jnp.max(s, axis=0, keepdims=True))
        alpha = jnp.exp(m_old - m_new)
        p = jnp.exp(s - m_new)
        l_ref[...] = alpha * l_ref[...] + jnp.sum(p, axis=0, keepdims=True)
        acc_ref[...] = alpha * acc_ref[...] + _dot(ct_ref[...], p.astype(BF16))
        m_ref[...] = m_new

    @pl.when(j < i)
    def _():
        step(False)

    @pl.when(j == i)
    def _():
        step(True)
        o_t = (acc_ref[...] / l_ref[...]).astype(BF16)
        y_t = jnp.concatenate([_dot(wuvt_ref[h], o_t[:, h * tq:(h + 1) * tq]) for h in range(nh)], axis=0)
        y_ref[...] = y_t.T


def _mla_attn_prompt(qf, rowsb, ct, wuvt, nb, t):
    tq = _pick(t, (256, 128))
    nq = t // tq
    pairs = [(i, j) for i in range(nq) for j in range(i + 1)]
    i_tab = jnp.asarray([p[0] for p in pairs], jnp.int32)
    j_tab = jnp.asarray([p[1] for p in pairs], jnp.int32)
    grid_spec = pltpu.PrefetchScalarGridSpec(
        num_scalar_prefetch=2,
        grid=(nb, len(pairs)),
        in_specs=[pl.BlockSpec((MLA_HEADS, tq, 256), lambda b, t_, it, jt: (0, b * nq + it[t_], 0)),
                  pl.BlockSpec((tq, 256), lambda b, t_, it, jt: (b * nq + jt[t_], 0)),
                  pl.BlockSpec((MLA_KV_RANK, tq), lambda b, t_, it, jt: (0, b * nq + jt[t_])),
                  pl.BlockSpec((MLA_HEADS, 64, 128), lambda b, t_, it, jt: (0, 0, 0))],
        out_specs=pl.BlockSpec((tq, 512), lambda b, t_, it, jt: (b * nq + it[t_], 0)),
        scratch_shapes=[pltpu.VMEM((1, MLA_HEADS * tq), F32), pltpu.VMEM((1, MLA_HEADS * tq), F32),
                        pltpu.VMEM((MLA_KV_RANK, MLA_HEADS * tq), F32)])
    return pl.pallas_call(
        functools.partial(_mla_attn_kernel, tq=tq),
        grid_spec=grid_spec,
        out_shape=jax.ShapeDtypeStruct((nb * t, 512), F32),
        compiler_params=_cparams(("parallel", "arbitrary")),
        name="mla_attn_prompt",
    )(i_tab, j_tab, qf, rowsb, ct, wuvt)


def _page_copy(cache_ref, buf_ref, sem_ref, row, slot, j):
    return pltpu.make_async_copy(cache_ref.at[row], buf_ref.at[slot, j], sem_ref.at[slot])


def _stream_step(pt_ref, cache_ref, buf_ref, sem_ref, *, layer, depth, pc):
    b = pl.program_id(0)
    c = pl.program_id(1)
    nb = pl.num_programs(0)
    nc = pl.num_programs(1)
    step = b * nc + c
    slot = step % 2

    def fetch(bb, cc, sl):
        for j in range(pc):
            row = pt_ref[bb, cc * pc + j] * depth + layer
            _page_copy(cache_ref, buf_ref, sem_ref, row, sl, j).start()

    @pl.when(step == 0)
    def _():
        fetch(b, c, slot)

    @pl.when(step + 1 < nb * nc)
    def _():
        wrap = c + 1 == nc
        fetch(jnp.where(wrap, b + 1, b), jnp.where(wrap, 0, c + 1), 1 - slot)

    for j in range(pc):
        _page_copy(cache_ref, buf_ref, sem_ref, 0, slot, j).wait()
    return slot


def _mla_decode_kernel(pt_ref, q_ref, new_ref, cache_ref, o_ref, buf_ref, sem_ref, m_ref, l_ref, acc_ref,
                       *, layer, depth, pc):
    c = pl.program_id(1)
    nc = pl.num_programs(1)
    slot = _stream_step(pt_ref, cache_ref, buf_ref, sem_ref, layer=layer, depth=depth, pc=pc)

    @pl.when(c == 0)
    def _():
        m_ref[...] = jnp.full_like(m_ref, NEG_BIG)
        l_ref[...] = jnp.zeros_like(l_ref)
        acc_ref[...] = jnp.zeros_like(acc_ref)

    q = q_ref[0]
    ql = jnp.broadcast_to(q[None, :, :MLA_KV_RANK], (pc, MLA_HEADS, MLA_KV_RANK))
    qr = jnp.broadcast_to(q[None, :, MLA_KV_RANK:MLA_ROW], (pc, MLA_HEADS, MLA_ROPE))
    kt = buf_ref[slot]
    kc = kt[:, :MLA_KV_RANK, :].astype(BF16)
    kr = kt[:, MLA_KV_RANK:, :].astype(BF16)
    s = (jnp.einsum('phk,pkn->phn', ql, kc, preferred_element_type=F32)
         + jnp.einsum('phk,pkn->phn', qr, kr, preferred_element_type=F32))
    m_old = m_ref[...]
    m_new = jnp.maximum(m_old, jnp.max(jnp.max(s, axis=0), axis=-1, keepdims=True))
    alpha = jnp.exp(m_old - m_new)
    p = jnp.exp(s - m_new[None])
    l_new = alpha * l_ref[...] + jnp.sum(jnp.sum(p, axis=0), axis=-1, keepdims=True)
    pv = jnp.einsum('phn,pcn->phc', p.astype(BF16), kc, preferred_element_type=F32)
    acc_new = alpha * acc_ref[...] + jnp.sum(pv, axis=0)
    m_ref[...] = m_new
    l_ref[...] = l_new
    acc_ref[...] = acc_new

    @pl.when(c == nc - 1)
    def _():
        new = new_ref[0].astype(F32)
        s_n = jnp.sum(q.astype(F32) * new, axis=-1, keepdims=True)
        m_f = jnp.maximum(m_new, s_n)
        al = jnp.exp(m_new - m_f)
        p_n = jnp.exp(s_n - m_f)
        l_f = al * l_new + p_n
        acc_f = al * acc_new + p_n.astype(BF16).astype(F32) * new[:, :MLA_KV_RANK]
        o_ref[0] = acc_f / l_f


def _mla_decode(page_table, qf_s, rowsb_s, cache, layer, depth):
    nb, n_pages = page_table.shape
    pc = _pick(n_pages, (128, 64, 32, 16, 8, 4, 2, 1))
    nc = n_pages // pc
    grid_spec = pltpu.PrefetchScalarGridSpec(
        num_scalar_prefetch=1,
        grid=(nb, nc),
        in_specs=[pl.BlockSpec((1, MLA_HEADS, 256), lambda b, c, pt: (b, 0, 0)),
                  pl.BlockSpec((1, 1, 256), lambda b, c, pt: (b, 0, 0)),
                  pl.BlockSpec(memory_space=pl.ANY)],
        out_specs=pl.BlockSpec((1, MLA_HEADS, MLA_KV_RANK), lambda b, c, pt: (b, 0, 0)),
        scratch_shapes=[pltpu.VMEM((2, pc, MLA_ROW, PAGE_SIZE), F32),
                        pltpu.SemaphoreType.DMA((2,)),
                        pltpu.VMEM((MLA_HEADS, 1), F32), pltpu.VMEM((MLA_HEADS, 1), F32),
                        pltpu.VMEM((MLA_HEADS, MLA_KV_RANK), F32)])
    return pl.pallas_call(
        functools.partial(_mla_decode_kernel, layer=layer, depth=depth, pc=pc),
        grid_spec=grid_spec,
        out_shape=jax.ShapeDtypeStruct((nb, MLA_HEADS, MLA_KV_RANK), F32),
        compiler_params=_cparams(("arbitrary", "arbitrary")),
        name="mla_decode",
    )(page_table, qf_s, rowsb_s.reshape(nb, 1, 256), cache)


def _mla_out_kernel(o_ref, wuv_ref, y_ref):
    for h in range(MLA_HEADS):
        y_ref[:, h * 64:(h + 1) * 64] = _dot(o_ref[h].astype(BF16), wuv_ref[h])


def _mla_out(o_hm, wuv):
    m = o_hm.shape[1]
    return pl.pallas_call(
        _mla_out_kernel,
        grid=(1,),
        in_specs=[pl.BlockSpec((MLA_HEADS, m, 128), lambda i: (0, 0, 0)),
                  pl.BlockSpec((MLA_HEADS, 128, 64), lambda i: (0, 0, 0))],
        out_specs=pl.BlockSpec((m, 512), lambda i: (0, 0)),
        out_shape=jax.ShapeDtypeStruct((m, 512), F32),
        compiler_params=_cparams(("arbitrary",)),
        name="mla_out",
    )(o_hm, wuv)


def _topk_select(gate, gate_row, n_valid, topk):
    nblk = gate.shape[0]
    blk = lax.broadcasted_iota(jnp.int32, gate.shape, 0)
    rank = jnp.zeros(gate.shape, F32)
    for m in range(nblk):
        g_m = gate_row(m)
        ahead = jnp.where(g_m > gate, 1.0, jnp.where((g_m == gate) & (m < blk), 1.0, 0.0))
        rank = rank + ahead * jnp.where(m < n_valid, 1.0, 0.0)
    return jnp.where((rank < topk) & (blk < n_valid), 1.0, 0.0)


def _moba_prompt_kernel(q_ref, kv_ref, slope_ref, y_ref, vt_ref, kb_ref, km_ref, gate_ref, sel_ref, *, nblk):
    i = pl.program_id(1)
    nq = MB_GROUP * MB_BLOCK
    scale = MB_HEAD_DIM ** -0.5

    @pl.when(i == 0)
    def _():
        for n in range(nblk):
            blk = kv_ref[n * MB_BLOCK:(n + 1) * MB_BLOCK, :]
            vt_ref[n] = blk[:, 128:256].T.astype(BF16)
            kb_ref[n * MB_BLOCK:(n + 1) * MB_BLOCK, :] = blk[:, 0:128].astype(BF16)
            km_ref[n:n + 1, :] = jnp.mean(blk[:, 0:128], axis=0, keepdims=True)

    q = q_ref[...]
    qoff = (lax.broadcasted_iota(jnp.int32, (1, nq), 1) % MB_BLOCK).astype(F32)
    koff = lax.broadcasted_iota(jnp.int32, (MB_BLOCK, 1), 0).astype(F32)
    outs = []
    for g in range(MB_KV_HEADS):
        qs = jnp.concatenate([q[:, (g * MB_GROUP + e) * 64:(g * MB_GROUP + e + 1) * 64]
                              for e in range(MB_GROUP)], axis=0)
        qsb = (qs * scale).astype(BF16)
        slope = slope_ref[g:g + 1, :]
        gate = _dot_nt(km_ref[:, g * 64:(g + 1) * 64], qs, precision=HIGHEST)
        gate_ref[...] = gate
        sel_ref[...] = _topk_select(gate, lambda m: gate_ref[m:m + 1, :], i, MB_TOPK)

        bias0 = slope * koff

        def scores(n):
            kb = kb_ref[pl.ds(pl.multiple_of(n * MB_BLOCK, MB_BLOCK), MB_BLOCK), g * 64:(g + 1) * 64]
            return _dot_nt(kb, qsb) + bias0

        def accumulate(n, s, row_bias, carry):
            m_old, l_old, acc = carry
            m_new = jnp.maximum(m_old, jnp.max(s, axis=0, keepdims=True) + row_bias)
            alpha = jnp.exp(m_old - m_new)
            p = jnp.exp(s - (m_new - row_bias))
            l_new = alpha * l_old + jnp.sum(p, axis=0, keepdims=True)
            vt = vt_ref[n][g * 64:(g + 1) * 64, :]
            acc = alpha * acc + _dot(vt, p.astype(BF16))
            return m_new, l_new, acc

        s_own = jnp.where(koff <= qoff, scores(i), NEG_BIG)
        init = (jnp.full((1, nq), NEG_BIG, F32), jnp.zeros((1, nq), F32), jnp.zeros((64, nq), F32))
        carry = accumulate(i, s_own, slope * (i * MB_BLOCK).astype(F32), init)

        def body(n, carry):
            row_bias = slope * (n * MB_BLOCK).astype(F32) + jnp.where(sel_ref[pl.ds(n, 1), :] > 0.5, 0.0, NEG_BIG)
            return accumulate(n, scores(n), row_bias, carry)

        _, l_f, acc = lax.fori_loop(0, i, body, carry)
        outs.append(acc / l_f)

    ot = jnp.concatenate(outs, axis=0).T
    for g in range(MB_KV_HEADS):
        for e in range(MB_GROUP):
            hh = g * MB_GROUP + e
            y_ref[:, hh * 64:(hh + 1) * 64] = ot[e * MB_BLOCK:(e + 1) * MB_BLOCK, g * 64:(g + 1) * 64]


def _moba_prompt(proj, slopes_row, nb, t):
    assert t % MB_BLOCK == 0
    nblk = t // MB_BLOCK
    return pl.pallas_call(
        functools.partial(_moba_prompt_kernel, nblk=nblk),
        grid=(nb, nblk),
        in_specs=[pl.BlockSpec((MB_BLOCK, 512), lambda b, i: (b * nblk + i, OFF_MQ // 512)),
                  pl.BlockSpec((t, 256), lambda b, i: (b, OFF_KV // 256)),
                  pl.BlockSpec((MB_KV_HEADS, MB_GROUP * MB_BLOCK), lambda b, i: (0, 0))],
        out_specs=pl.BlockSpec((MB_BLOCK, 512), lambda b, i: (b * nblk + i, 0)),
        out_shape=jax.ShapeDtypeStruct((nb * t, 512), F32),
        scratch_shapes=[pltpu.VMEM((nblk, 128, MB_BLOCK), BF16), pltpu.VMEM((t, 128), BF16),
                        pltpu.VMEM((nblk, 128), F32),
                        pltpu.VMEM((nblk, MB_GROUP * MB_BLOCK), F32), pltpu.VMEM((nblk, MB_GROUP * MB_BLOCK), F32)],
        compiler_params=_cparams(("parallel", "arbitrary")),
        name="moba_prompt",
    )(proj, proj, slopes_row)


def _moba_keys_kernel(pt_ref, q_ref, knew_ref, slope_ref, cache_ref, p_ref, pown_ref, idx_ref,
                      buf_ref, sem_ref, *, layer, depth, pc, past_len):
    slot = _stream_step(pt_ref, cache_ref, buf_ref, sem_ref, layer=layer, depth=depth, pc=pc)
    ppb = MB_BLOCK // PAGE_SIZE
    nblk = pc // ppb
    scale = MB_HEAD_DIM ** -0.5
    q = q_ref[0]
    qb = (q * scale).astype(BF16)
    rep = lambda x, n: jnp.broadcast_to(x[None], (n,) + x.shape)
    bmm = lambda a, b: jnp.einsum('phf,pfn->phn', a, b, preferred_element_type=F32)
    kt = buf_ref[slot]
    s = bmm(rep(qb, pc), kt.astype(BF16))

    kt4 = kt.reshape(nblk, ppb, 128, PAGE_SIZE)
    ks = kt4[:, 0]
    for e in range(1, ppb):
        ks = ks + kt4[:, e]
    q_hi, q_lo = _split_bf16(q)
    ks_hi, ks_lo = _split_bf16(ks)
    g3 = bmm(rep(q_hi, nblk), ks_hi) + bmm(rep(q_hi, nblk), ks_lo) + bmm(rep(q_lo, nblk), ks_hi)
    gate = jnp.sum(g3, axis=-1, keepdims=True)

    blk_f = lax.broadcasted_iota(jnp.int32, gate.shape, 0).astype(F32)
    sel = jnp.zeros(gate.shape, F32)
    picks = []
    for _ in range(MB_TOPK):
        top = jnp.max(gate, axis=0, keepdims=True)
        pick = jnp.min(jnp.where(gate == top, blk_f, float(nblk)), axis=0, keepdims=True)
        hit = blk_f == pick
        sel = jnp.where(hit, 1.0, sel)
        gate = jnp.where(hit, NEG_BIG, gate)
        picks.append(pick)
    idx_ref[0] = jnp.concatenate(picks, axis=0).astype(jnp.int32)

    shp = (nblk, ppb, MB_HEADS, PAGE_SIZE)
    pos = (lax.broadcasted_iota(jnp.int32, shp, 0) * MB_BLOCK + lax.broadcasted_iota(jnp.int32, shp, 1) * PAGE_SIZE
           + lax.broadcasted_iota(jnp.int32, shp, 3))
    dist = (past_len - pos).astype(F32)
    slope = slope_ref[...]
    s_m = jnp.where(sel[:, None] > 0.5, s.reshape(shp) - slope[None, None] * dist, NEG_BIG)
    s_own = jnp.sum(qb.astype(F32) * knew_ref[0].astype(BF16).astype(F32), axis=-1, keepdims=True)
    m = jnp.maximum(jnp.max(jnp.max(s_m, axis=(0, 1)), axis=-1, keepdims=True), s_own)
    e = jnp.exp(s_m - m[None, None])
    e_own = jnp.exp(s_own - m)
    inv = 1.0 / (jnp.sum(jnp.sum(e, axis=(0, 1)), axis=-1, keepdims=True) + e_own)
    p_ref[0] = (e * inv[None, None]).reshape(pc, MB_HEADS, PAGE_SIZE)
    pown_ref[0] = jnp.broadcast_to(e_own * inv, (MB_HEADS, 128))


def _moba_keys(page_table, q_pad, k_new, slopes_col, cache, layer, depth):
    nb, n_pages = page_table.shape
    pc = n_pages
    past_len = n_pages * PAGE_SIZE
    assert past_len // MB_BLOCK >= MB_TOPK
    grid_spec = pltpu.PrefetchScalarGridSpec(
        num_scalar_prefetch=1,
        grid=(nb, 1),
        in_specs=[pl.BlockSpec((1, MB_HEADS, 128), lambda b, c, pt: (b, 0, 0)),
                  pl.BlockSpec((1, 1, 128), lambda b, c, pt: (b, 0, 0)),
                  pl.BlockSpec((MB_HEADS, 1), lambda b, c, pt: (0, 0)),
                  pl.BlockSpec(memory_space=pl.ANY)],
        out_specs=[pl.BlockSpec((1, pc, MB_HEADS, PAGE_SIZE), lambda b, c, pt: (b, 0, 0, 0)),
                   pl.BlockSpec((1, MB_HEADS, 128), lambda b, c, pt: (b, 0, 0)),
                   pl.BlockSpec((1, MB_TOPK, MB_HEADS, 1), lambda b, c, pt: (b, 0, 0, 0))],
        scratch_shapes=[pltpu.VMEM((2, pc, 128, PAGE_SIZE), F32),
                        pltpu.SemaphoreType.DMA((2,))])
    return pl.pallas_call(
        functools.partial(_moba_keys_kernel, layer=layer, depth=depth, pc=pc, past_len=past_len),
        grid_spec=grid_spec,
        out_shape=[jax.ShapeDtypeStruct((nb, pc, MB_HEADS, PAGE_SIZE), F32),
                   jax.ShapeDtypeStruct((nb, MB_HEADS, 128), F32),
                   jax.ShapeDtypeStruct((nb, MB_TOPK, MB_HEADS, 1), jnp.int32)],
        compiler_params=_cparams(("arbitrary", "arbitrary")),
        name="moba_keys",
    )(page_table, q_pad, k_new.reshape(nb, 1, 128), slopes_col, cache)


def _moba_values_kernel(pt_ref, idx_ref, p_ref, pown_ref, vnew_ref, cache_ref, o_ref, buf_ref, sem_ref,
                        *, layer, depth):
    b = pl.program_id(0)
    nb = pl.num_programs(0)
    slot = b % 2
    ppb = MB_BLOCK // PAGE_SIZE
    n_half = MB_HEADS * MB_TOPK * ppb

    def half_page(row, h, sl, i):
        return pltpu.make_async_copy(cache_ref.at[row, pl.ds((h // MB_GROUP) * 64, 64), :],
                                     buf_ref.at[sl, i], sem_ref.at[sl])

    def fetch(bb, sl):
        for h in range(MB_HEADS):
            for j in range(MB_TOPK):
                blk = idx_ref[bb, h * MB_TOPK + j]
                for e in range(ppb):
                    row = pt_ref[bb, blk * ppb + e] * depth + layer
                    half_page(row, h, sl, (h * MB_TOPK + j) * ppb + e).start()

    @pl.when(b == 0)
    def _():
        fetch(b, slot)

    @pl.when(b + 1 < nb)
    def _():
        fetch(b + 1, 1 - slot)

    for i in range(n_half):
        half_page(0, 0, slot, i).wait()

    row_id = lax.broadcasted_iota(jnp.int32, (MB_HEADS, 1), 0)
    acc = jnp.zeros((MB_HEADS, 64), F32)
    for h in range(MB_HEADS):
        vts, prs = [], []
        for j in range(MB_TOPK):
            blk = idx_ref[b, h * MB_TOPK + j]
            for e in range(ppb):
                vts.append(buf_ref[slot, (h * MB_TOPK + j) * ppb + e])
                prs.append(p_ref[0, blk * ppb + e, h:h + 1, :])
        vt = jnp.concatenate(vts, axis=1).astype(BF16)
        pr = jnp.concatenate(prs, axis=1)
        lhs = jnp.where(row_id == h, pr, 0.0).astype(BF16)
        acc = acc + _dot_nt(lhs, vt)
    v_new = vnew_ref[0].astype(BF16).astype(F32)
    v8 = jnp.where(row_id < MB_GROUP, v_new[:, 0:64], v_new[:, 64:128])
    o_ref[0] = acc + pown_ref[0][:, 0:64].astype(BF16).astype(F32) * v8


def _moba_values(page_table, idx, p, p_own, v_new, cache, layer, depth):
    nb, n_pages = page_table.shape
    ppb = MB_BLOCK // PAGE_SIZE
    n_half = MB_HEADS * MB_TOPK * ppb
    grid_spec = pltpu.PrefetchScalarGridSpec(
        num_scalar_prefetch=2,
        grid=(nb,),
        in_specs=[pl.BlockSpec((1, n_pages, MB_HEADS, PAGE_SIZE), lambda b, pt, ix: (b, 0, 0, 0)),
                  pl.BlockSpec((1, MB_HEADS, 128), lambda b, pt, ix: (b, 0, 0)),
                  pl.BlockSpec((1, 1, 128), lambda b, pt, ix: (b, 0, 0)),
                  pl.BlockSpec(memory_space=pl.ANY)],
        out_specs=pl.BlockSpec((1, MB_HEADS, 64), lambda b, pt, ix: (b, 0, 0)),
        scratch_shapes=[pltpu.VMEM((2, n_half, 64, PAGE_SIZE), F32),
                        pltpu.SemaphoreType.DMA((2,))])
    return pl.pallas_call(
        functools.partial(_moba_values_kernel, layer=layer, depth=depth),
        grid_spec=grid_spec,
        out_shape=jax.ShapeDtypeStruct((nb, MB_HEADS, 64), F32),
        compiler_params=_cparams(("arbitrary",)),
        name="moba_values",
    )(page_table, idx, p, p_own, v_new.reshape(nb, 1, 128), cache)


def _merge_kernel(ya_ref, yb_ref, yc_ref, gate_ref, mg_ref, x_ref, wb_ref, wo_ref, g_ref, b_ref, o_ref, *, alpha):
    merged = None
    for n, y_ref in enumerate((ya_ref, yb_ref, yc_ref)):
        gcol = gate_ref[:, n * BR_WIDTH:(n + 1) * BR_WIDTH]
        o = y_ref[...] * (gcol * _sigmoid(gcol))
        br = _dot(o.astype(BF16), wb_ref[n])
        term = br * _sigmoid(mg_ref[:, n * D_MODEL:(n + 1) * D_MODEL])
        merged = term if merged is None else merged + term
    z = alpha * x_ref[...] + _dot(merged.astype(BF16), wo_ref[...])
    mu = jnp.mean(z, axis=-1, keepdims=True)
    zc = z - mu
    var = jnp.mean(zc * zc, axis=-1, keepdims=True)
    o_ref[...] = zc * lax.rsqrt(var + 1e-5) * g_ref[...] + b_ref[...]


def _merge(ya, yb, yc, proj, x, wb, wo, ln_g, ln_b, alpha):
    m = x.shape[0]
    tm = _pick(m, (256, 128))
    row = lambda w, j=0: pl.BlockSpec((tm, w), lambda i: (i, j))
    return pl.pallas_call(
        functools.partial(_merge_kernel, alpha=alpha),
        grid=(m // tm,),
        in_specs=[row(512), row(512), row(512),
                  row(1536, OFF_GATE // 1536), row(3072, OFF_MERGE // 3072), row(D_MODEL),
                  pl.BlockSpec((N_BRANCH, BR_WIDTH, D_MODEL), lambda i: (0, 0, 0)),
                  pl.BlockSpec((D_MODEL, D_MODEL), lambda i: (0, 0)),
                  pl.BlockSpec((1, D_MODEL), lambda i: (0, 0)),
                  pl.BlockSpec((1, D_MODEL), lambda i: (0, 0))],
        out_specs=row(D_MODEL),
        out_shape=jax.ShapeDtypeStruct((m, D_MODEL), F32),
        compiler_params=_cparams(("parallel",)),
        name="merge",
    )(ya, yb, yc, proj, proj, x, wb, wo, ln_g, ln_b)


def _rope_tables(pos):
    half = MLA_ROPE // 2
    inv = ROPE_THETA ** (-jnp.arange(half, dtype=F32) / half)
    ang = pos.astype(F32)[:, None] * inv[None, :]
    return jnp.tile(jnp.cos(ang), (1, MLA_HEADS)), jnp.tile(jnp.sin(ang), (1, MLA_HEADS))


def _permute_w_in(w):
    d = w.shape[0]
    z = lambda n: jnp.zeros((d, n), w.dtype)
    return jnp.concatenate([w[:, 0:1664], z(128), w[:, 2592:2848], w[:, 1664:2080], z(96),
                            w[:, 2080:2592], w[:, 4384:7456], w[:, 2848:4384]], axis=1).astype(BF16)


def kernel(x_prompt, x_sample, cache_mla, cache_moba_k, cache_moba_v, state_wkv, state_shift, page_table, w_in, rw_mu, rw_w0, rw_w2, rw_a0, rw_a2, rw_k_k, rw_k_a, rw_r_k, rw_gn_g, rw_gn_b, mla_q_norm, mla_w_uq, mla_kv_norm, mla_w_uk, mla_w_uv, w_branch, w_out, ln_g, ln_b):
    bp, tp, d = x_prompt.shape
    bs, ts, _ = x_sample.shape
    assert ts == 1 and d == D_MODEL
    depth = w_in.shape[0]
    n_pool = cache_mla.shape[0]
    n_pages = page_table.shape[1]
    past_len = n_pages * PAGE_SIZE
    assert past_len % MB_BLOCK == 0
    alpha = (2 * depth) ** 0.25

    cache_mla2 = jnp.transpose(cache_mla, (0, 1, 3, 2)).reshape(n_pool * depth, MLA_ROW, PAGE_SIZE)
    cache_k2 = jnp.transpose(cache_moba_k, (0, 1, 3, 4, 2)).reshape(n_pool * depth, 128, PAGE_SIZE)
    cache_v2 = jnp.transpose(cache_moba_v, (0, 1, 3, 4, 2)).reshape(n_pool * depth, 128, PAGE_SIZE)

    cos_p, sin_p = _rope_tables(jnp.tile(jnp.arange(tp), bp))
    cos_s, sin_s = _rope_tables(jnp.full((bs,), past_len))
    slopes = 2.0 ** (-8.0 * jnp.arange(1, MB_HEADS + 1, dtype=F32) / MB_HEADS)
    slopes_row = jnp.repeat(slopes.reshape(MB_KV_HEADS, MB_GROUP), MB_BLOCK, axis=1)
    slopes_col = slopes.reshape(MB_HEADS, 1)
    half = MLA_ROPE // 2
    hd = MLA_NOPE + MLA_ROPE

    xp = x_prompt.reshape(bp * tp, d)
    xs = x_sample.reshape(bs, d)
    zero_shift = jnp.zeros((bp, RW_SHIFT_W), F32)
    zero_wkv = jnp.zeros((bp, RW_HEADS, 64, 64), F32)

    mla_p, mla_s, kp, ksm, vp, vsm, wp, wsm, sp, ssm = [], [], [], [], [], [], [], [], [], []
    for l in range(depth):
        w_perm = _permute_w_in(w_in[l])
        r2 = lambda a: a.reshape(1, -1)
        rw = (r2(rw_mu[l]), r2(rw_w0[l]), rw_w2[l].astype(BF16), r2(rw_a0[l]), rw_a2[l].astype(BF16),
              r2(rw_k_k[l]), r2(rw_k_a[l]), r2(rw_r_k[l]), r2(rw_gn_g[l]), r2(rw_gn_b[l]))
        wuq3 = mla_w_uq[l].reshape(MLA_Q_RANK, MLA_HEADS, hd)
        wuq = jnp.concatenate([wuq3[:, :, :MLA_NOPE].reshape(MLA_Q_RANK, -1),
                               wuq3[:, :, MLA_NOPE:MLA_NOPE + half].reshape(MLA_Q_RANK, -1),
                               wuq3[:, :, MLA_NOPE + half:].reshape(MLA_Q_RANK, -1)], axis=1).astype(BF16)
        wukt = jnp.transpose(mla_w_uk[l], (1, 2, 0)).astype(BF16)
        wuv = jnp.transpose(mla_w_uv[l], (1, 0, 2)).astype(BF16)
        wuvt = jnp.transpose(mla_w_uv[l], (1, 2, 0)).astype(BF16)
        mw = (r2(mla_q_norm[l]), wuq, r2(mla_kv_norm[l]), wukt)
        wb = w_branch[l].astype(BF16)
        wo = w_out[l].astype(BF16)
        lg, lb = r2(ln_g[l]), r2(ln_b[l])

        proj = _inproj(xp, w_perm)
        ya, wkv_new = _rwkv_prompt(proj, zero_shift, zero_wkv, rw, bp, tp)
        rows, rowsb, ct, qf = _mla_prep(proj, cos_p, sin_p, mw)
        yb = _mla_attn_prompt(qf, rowsb, ct, wuvt, bp, tp)
        yc = _moba_prompt(proj, slopes_row, bp, tp)
        xp_new = _merge(ya, yb, yc, proj, xp, wb, wo, lg, lb, alpha)
        mla_p.append(rows.reshape(bp, tp, MLA_ROW))
        kp.append(proj[:, OFF_KV:OFF_KV + 128].reshape(bp, tp, MB_KV_HEADS, MB_HEAD_DIM))
        vp.append(proj[:, OFF_KV + 128:OFF_KV + 256].reshape(bp, tp, MB_KV_HEADS, MB_HEAD_DIM))
        wp.append(wkv_new)
        sp.append(proj.reshape(bp, tp, PROJ_W)[:, tp - 1, :RW_SHIFT_W])
        xp = xp_new

        proj = _inproj(xs, w_perm)
        ya, wkv_t = _rwkv_sample(proj, state_shift[l], jnp.transpose(state_wkv[l], (1, 2, 3, 0)), rw)
        wkv_new = jnp.transpose(wkv_t, (3, 0, 1, 2))
        rows, rowsb, _, qf = _mla_prep(proj, cos_s, sin_s, mw)
        o_lat = _mla_decode(page_table, jnp.transpose(qf, (1, 0, 2)), rowsb, cache_mla2, l, depth)
        yb = _mla_out(jnp.transpose(o_lat, (1, 0, 2)), wuv)
        k_new = proj[:, OFF_KV:OFF_KV + 128]
        v_new = proj[:, OFF_KV + 128:OFF_KV + 256]
        q4 = proj[:, OFF_MQ:OFF_MQ + 512].reshape(bs, MB_KV_HEADS, MB_GROUP, MB_HEAD_DIM)
        q_pad = jnp.concatenate(
            [jnp.pad(q4[:, g], ((0, 0), (0, 0), (g * 64, (MB_KV_HEADS - 1 - g) * 64))) for g in range(MB_KV_HEADS)],
            axis=1)
        p_att, p_own, idx = _moba_keys(page_table, q_pad, k_new, slopes_col, cache_k2, l, depth)
        idx = jnp.transpose(idx[..., 0], (0, 2, 1)).reshape(bs, MB_HEADS * MB_TOPK)
        yc = _moba_values(page_table, idx, p_att, p_own, v_new, cache_v2, l, depth).reshape(bs, MB_HEADS * 64)
        xs_new = _merge(ya, yb, yc, proj, xs, wb, wo, lg, lb, alpha)
        mla_s.append(rows.reshape(bs, 1, MLA_ROW))
        ksm.append(k_new.reshape(bs, 1, MB_KV_HEADS, MB_HEAD_DIM))
        vsm.append(v_new.reshape(bs, 1, MB_KV_HEADS, MB_HEAD_DIM))
        wsm.append(wkv_new)
        ssm.append(proj[:, :RW_SHIFT_W])
        xs = xs_new

    return (xp.reshape(bp, tp, d), xs.reshape(bs, 1, d),
            jnp.stack(mla_p, 1), jnp.stack(mla_s, 1),
            jnp.stack(kp, 1), jnp.stack(ksm, 1), jnp.stack(vp, 1), jnp.stack(vsm, 1),
            jnp.stack(wp, 0), jnp.stack(wsm, 0), jnp.stack(sp, 0), jnp.stack(ssm, 0))
```

```python
import functools
import math

import jax
import jax.numpy as jnp
from jax import lax
from jax.experimental import pallas as pl
from jax.experimental.pallas import tpu as pltpu

F32 = jnp.float32
BF16 = jnp.bfloat16
HIGHEST = lax.Precision.HIGHEST

D_MODEL = 1024
PAGE_SIZE = 128
RW_HEADS = 8
RW_HEAD_DIM = 64
RW_WIDTH = 512
RW_DECAY_RANK = 64
RW_SHIFT_W = 1664
RW_GN_EPS = 64e-5
MLA_HEADS = 8
MLA_NOPE = 64
MLA_ROPE = 32
MLA_V = 64
MLA_Q_RANK = 256
MLA_KV_RANK = 128
MLA_ROW = 160
ROPE_THETA = 10000.0
MB_HEADS = 8
MB_KV_HEADS = 2
MB_HEAD_DIM = 64
MB_GROUP = 4
MB_BLOCK = 256
MB_TOPK = 3
N_BRANCH = 3
BR_WIDTH = 512
NEG_BIG = -1e30

OFF_RW = 0
OFF_KV = 1792
OFF_MLA = 2048
OFF_MQ = 2560
OFF_MERGE = 3072
OFF_GATE = 6144
PROJ_W = 7680

VMEM_LIMIT = 56 * 1024 * 1024


def _cparams(sem, vmem=VMEM_LIMIT):
    return pltpu.CompilerParams(dimension_semantics=sem, vmem_limit_bytes=vmem)


def _pick(n, prefs):
    for p in prefs:
        if n % p == 0:
            return p
    return n


def _dot(a, b, **kw):
    return jnp.dot(a, b, preferred_element_type=F32, **kw)


def _dot_nt(a, b, **kw):
    return lax.dot_general(a, b, (((1,), (1,)), ((), ())), preferred_element_type=F32, **kw)


def _split_bf16(x):
    hi = x.astype(BF16)
    lo = (x - hi.astype(F32)).astype(BF16)
    return hi, lo


def _mm_any(f, a, b, passes):
    if passes == 1:
        return f(a.astype(BF16), b.astype(BF16))
    a_hi, a_lo = _split_bf16(a)
    b_hi, b_lo = _split_bf16(b)
    return f(a_hi, b_hi) + f(a_hi, b_lo) + f(a_lo, b_hi)


def _mm(a, b, passes):
    return _mm_any(_dot, a, b, passes)


def _mm_nt(a, b, passes):
    return _mm_any(_dot_nt, a, b, passes)


def _bmm(a, b, passes):
    return _mm_any(lambda x, y: jnp.einsum('hij,hjk->hik', x, y, preferred_element_type=F32), a, b, passes)


def _bmm_nt(a, b, passes):
    return _mm_any(lambda x, y: jnp.einsum('hik,hjk->hij', x, y, preferred_element_type=F32), a, b, passes)


RW_PASSES_A = 1
RW_PASSES_INV = 1


def _sigmoid(x):
    return 1.0 / (1.0 + jnp.exp(-x))


def _inproj_kernel(x_ref, w_ref, o_ref, xb_ref):
    @pl.when(pl.program_id(1) == 0)
    def _():
        xb_ref[...] = x_ref[...].astype(BF16)

    o_ref[...] = _dot(xb_ref[...], w_ref[...])


def _inproj(x, w_perm):
    m, d = x.shape
    tm = _pick(m, (1024, 512, 256, 128))
    tn = 1280
    return pl.pallas_call(
        _inproj_kernel,
        grid=(m // tm, PROJ_W // tn),
        in_specs=[pl.BlockSpec((tm, d), lambda i, j: (i, 0)),
                  pl.BlockSpec((d, tn), lambda i, j: (0, j))],
        out_specs=pl.BlockSpec((tm, tn), lambda i, j: (i, j)),
        out_shape=jax.ShapeDtypeStruct((m, PROJ_W), F32),
        scratch_shapes=[pltpu.VMEM((tm, d), BF16)],
        compiler_params=_cparams(("parallel", "arbitrary")),
        name="inproj",
    )(x, w_perm)


def _rwkv_prep(cols, prev, mu, w0, w2b, a0, a2b, k_k, k_a):
    mixed = cols + (prev - cols) * mu
    r = mixed[:, 0:512]
    k = mixed[:, 512:1024]
    v = mixed[:, 1024:1536]
    wd = mixed[:, 1536:1600]
    ad = mixed[:, 1600:1664]
    zw = w0 + _dot(jnp.tanh(wd).astype(BF16), w2b)
    nz = -zw
    softplus = jnp.maximum(nz, 0.0) + jnp.log(1.0 + jnp.exp(-jnp.abs(nz)))
    lw = -jnp.exp(-softplus - 0.5)
    a = _sigmoid(a0 + _dot(ad.astype(BF16), a2b))
    kk = k * k_k
    k_mod = k * (1.0 + (a - 1.0) * k_a)
    return r, k_mod, v, lw, a, kk


def _head_norm(kk_h):
    ss = jnp.sum(kk_h * kk_h, axis=-1, keepdims=True)
    return kk_h * lax.rsqrt(jnp.maximum(ss, 1e-24))


def _group_norm_bonus(y_h, r_h, kmod_h, v_h, rk_h, g_h, b_h):
    ym = jnp.mean(y_h, axis=-1, keepdims=True)
    yc = y_h - ym
    yv = jnp.mean(yc * yc, axis=-1, keepdims=True)
    yn = yc * lax.rsqrt(yv + RW_GN_EPS) * g_h + b_h
    bonus = jnp.sum(r_h * kmod_h * rk_h, axis=-1, keepdims=True) * v_h
    return yn + bonus


def _rwkv_chunk_kernel(p_ref, shift0_ref, wkv0_ref, mu_ref, w0_ref, w2_ref, a0_ref, a2_ref,
                       kk_ref, ka_ref, rk_ref, gg_ref, gb_ref,
                       y_ref, wkv_ref, carry_ref, s_ref, *, chunk):
    c = pl.program_id(1)
    nc = pl.num_programs(1)
    C = chunk

    @pl.when(c == 0)
    def _():
        carry_ref[...] = shift0_ref[0]
        s_ref[...] = wkv0_ref[0]

    cols = p_ref[...]
    rows = lax.broadcasted_iota(jnp.int32, cols.shape, 0)
    prev = jnp.where(rows == 0, carry_ref[...], pltpu.roll(cols, 1, 0))
    carry_ref[...] = cols[C - 1:C, :]

    r, k_mod, v, lw, a, kk = _rwkv_prep(cols, prev, mu_ref[...], w0_ref[...], w2_ref[...],
                                        a0_ref[...], a2_ref[...], kk_ref[...], ka_ref[...])

    ti = lax.broadcasted_iota(jnp.int32, (C, C), 0)
    si = lax.broadcasted_iota(jnp.int32, (C, C), 1)
    lower = ti >= si
    strict = ti > si
    eye = (ti == si).astype(F32)
    tri = lower.astype(BF16)
    lw_hi = lw.astype(BF16)
    lw_r = lw - lw_hi.astype(F32)
    lw_mid = lw_r.astype(BF16)
    lw_lo = (lw_r - lw_mid.astype(F32)).astype(BF16)
    cum = _dot(tri, lw_hi) + _dot(tri, lw_mid) + _dot(tri, lw_lo)
    p_incl = jnp.exp(cum)
    p_excl = jnp.exp(cum - lw)
    p_inv = jnp.exp(-cum)
    p_end = p_incl[C - 1:C, :]

    base = 16
    diag_mask = (ti // base) == (si // base)
    off_masks = []
    size = base
    while size < C:
        off_masks.append(((ti // (2 * size)) == (si // (2 * size))) & ((ti // size) != (si // size)))
        size *= 2
    heads = lambda x: jnp.stack([x[:, h * 64:(h + 1) * 64] for h in range(RW_HEADS)], axis=0)
    kk3 = heads(kk)
    kk3 = kk3 * lax.rsqrt(jnp.maximum(jnp.sum(kk3 * kk3, axis=-1, keepdims=True), 1e-24))
    v3, r3, km3 = heads(v), heads(r), heads(k_mod)
    pinv3 = heads(p_inv)
    at = -(kk3 * heads(p_excl))
    bt = kk3 * heads(a) * pinv3
    kt = km3 * pinv3
    rt = r3 * heads(p_incl)
    s0 = s_ref[...]

    lhs = jnp.concatenate([at, rt], axis=1)
    rhs = jnp.concatenate([bt, kt], axis=1)
    mx = _bmm_nt(lhs, rhs, RW_PASSES_A)
    a_ab = jnp.where(strict[None], mx[:, :C, :C], 0.0)
    a_ak = jnp.where(strict[None], mx[:, :C, C:], 0.0)
    r_b = jnp.where(lower[None], mx[:, C:, :C], 0.0)
    r_k = jnp.where(lower[None], mx[:, C:, C:], 0.0)

    npow = jnp.where(diag_mask[None], a_ab, 0.0)
    inv = eye[None] + npow
    for _ in range(3):
        npow = _bmm(npow, npow, RW_PASSES_INV)
        inv = inv + _bmm(npow, inv, RW_PASSES_INV)
    for off_mask in off_masks:
        inv = inv + _bmm(inv, _bmm(jnp.where(off_mask[None], a_ab, 0.0), inv, RW_PASSES_INV), RW_PASSES_INV)

    sa = _bmm_nt(lhs, s0, RW_PASSES_A)
    rhs_u = sa[:, :C] + _bmm(a_ak, v3, RW_PASSES_A)
    u = _bmm(inv, rhs_u, RW_PASSES_INV)
    y3 = sa[:, C:] + _bmm(r_b, u, RW_PASSES_A) + _bmm(r_k, v3, RW_PASSES_A)

    for h in range(RW_HEADS):
        sl = slice(h * 64, (h + 1) * 64)
        pe = p_end[:, sl]
        uv_t = jnp.concatenate([u[h], v3[h]], axis=1).T
        bk = jnp.concatenate([bt[h] * pe, kt[h] * pe], axis=1)
        upd = _mm(uv_t, bk, RW_PASSES_A)
        s_ref[h] = s0[h] * pe + upd[:64, :64] + upd[64:, 64:]
        y_ref[:, sl] = _group_norm_bonus(y3[h], r3[h], km3[h], v3[h], rk_ref[:, sl], gg_ref[:, sl], gb_ref[:, sl])

    @pl.when(c == nc - 1)
    def _():
        wkv_ref[0] = s_ref[...]


def _rwkv_prompt(proj, shift0, wkv0, rw, nb, t):
    chunk = 128
    assert t % chunk == 0
    nc = t // chunk
    vec = lambda n: pl.BlockSpec((1, n), lambda b, c: (0, 0))
    mat = lambda a, b_: pl.BlockSpec((a, b_), lambda b, c: (0, 0))
    return pl.pallas_call(
        functools.partial(_rwkv_chunk_kernel, chunk=chunk),
        grid=(nb, nc),
        in_specs=[pl.BlockSpec((chunk, RW_SHIFT_W), lambda b, c: (b * nc + c, 0)),
                  pl.BlockSpec((1, 1, RW_SHIFT_W), lambda b, c: (b, 0, 0)),
                  pl.BlockSpec((1, RW_HEADS, 64, 64), lambda b, c: (b, 0, 0, 0)),
                  vec(RW_SHIFT_W), vec(512), mat(64, 512), vec(512), mat(64, 512),
                  vec(512), vec(512), vec(512), vec(512), vec(512)],
        out_specs=[pl.BlockSpec((chunk, 512), lambda b, c: (b * nc + c, 0)),
                   pl.BlockSpec((1, RW_HEADS, 64, 64), lambda b, c: (b, 0, 0, 0))],
        out_shape=[jax.ShapeDtypeStruct((nb * t, 512), F32),
                   jax.ShapeDtypeStruct((nb, RW_HEADS, 64, 64), F32)],
        scratch_shapes=[pltpu.VMEM((1, RW_SHIFT_W), F32), pltpu.VMEM((RW_HEADS, 64, 64), F32)],
        compiler_params=_cparams(("parallel", "arbitrary")),
        name="rwkv_prompt",
    )(proj, shift0.reshape(nb, 1, RW_SHIFT_W), wkv0, *rw)


def _rwkv_step_kernel(p_ref, shift0_ref, wkv0_ref, mu_ref, w0_ref, w2_ref, a0_ref, a2_ref,
                      kk_ref, ka_ref, rk_ref, gg_ref, gb_ref,
                      y_ref, wkv_ref, r_s, k_s, v_s, wt_s, kt_s, kkt_s, bt_s, rt_s, vt_s, yt_s):
    h = pl.program_id(0)
    nh = pl.num_programs(0)

    @pl.when(h == 0)
    def _():
        r, k_mod, v, lw, a, kk = _rwkv_prep(p_ref[:, :RW_SHIFT_W], shift0_ref[...], mu_ref[...], w0_ref[...],
                                            w2_ref[...], a0_ref[...], a2_ref[...], kk_ref[...], ka_ref[...])
        kkn = jnp.concatenate([_head_norm(kk[:, g * 64:(g + 1) * 64]) for g in range(RW_HEADS)], axis=1)
        r_s[...] = r
        k_s[...] = k_mod
        v_s[...] = v
        wt_s[...] = jnp.exp(lw).T
        kt_s[...] = k_mod.T
        kkt_s[...] = kkn.T
        bt_s[...] = (kkn * a).T
        rt_s[...] = r.T
        vt_s[...] = v.T

    base = pl.multiple_of(h * 64, 64)
    w_t = wt_s[pl.ds(base, 64), :]
    k_t = kt_s[pl.ds(base, 64), :]
    kk_t = kkt_s[pl.ds(base, 64), :]
    b_t = bt_s[pl.ds(base, 64), :]
    r_t = rt_s[pl.ds(base, 64), :]

    def body(vi, carry):
        s0 = wkv0_ref[0, vi]
        sa = -jnp.sum(s0 * kk_t, axis=0, keepdims=True)
        s1 = s0 * w_t + sa * b_t + vt_s[pl.ds(base + vi, 1), :] * k_t
        wkv_ref[0, vi] = s1
        yt_s[pl.ds(base + vi, 1), :] = jnp.sum(s1 * r_t, axis=0, keepdims=True)
        return carry

    lax.fori_loop(0, RW_HEAD_DIM, body, 0)

    @pl.when(h == nh - 1)
    def _():
        y = yt_s[...].T
        for g in range(RW_HEADS):
            sl = slice(g * 64, (g + 1) * 64)
            y_ref[:, sl] = _group_norm_bonus(y[:, sl], r_s[:, sl], k_s[:, sl], v_s[:, sl],
                                             rk_ref[:, sl], gg_ref[:, sl], gb_ref[:, sl])


def _rwkv_sample(proj, shift0, wkv0_t, rw):
    nb = proj.shape[0]
    vec = lambda n: pl.BlockSpec((1, n), lambda i: (0, 0))
    mat = lambda a, b_: pl.BlockSpec((a, b_), lambda i: (0, 0))
    sq = pltpu.VMEM((nb, 512), F32)
    st = pltpu.VMEM((512, nb), F32)
    return pl.pallas_call(
        _rwkv_step_kernel,
        grid=(RW_HEADS,),
        in_specs=[pl.BlockSpec((nb, 1792), lambda i: (0, 0)),
                  mat(nb, RW_SHIFT_W),
                  pl.BlockSpec((1, 64, 64, nb), lambda i: (i, 0, 0, 0)),
                  vec(RW_SHIFT_W), vec(512), mat(64, 512), vec(512), mat(64, 512),
                  vec(512), vec(512), vec(512), vec(512), vec(512)],
        out_specs=[pl.BlockSpec((nb, 512), lambda i: (0, 0)),
                   pl.BlockSpec((1, 64, 64, nb), lambda i: (i, 0, 0, 0))],
        out_shape=[jax.ShapeDtypeStruct((nb, 512), F32),
                   jax.ShapeDtypeStruct((RW_HEADS, 64, 64, nb), F32)],
        scratch_shapes=[sq, sq, sq, st, st, st, st, st, st, st],
        compiler_params=_cparams(("arbitrary",)),
        name="rwkv_sample",
    )(proj, shift0, wkv0_t, *rw)


def _rms(x, g):
    return x * lax.rsqrt(jnp.mean(x * x, axis=-1, keepdims=True) + 1e-6) * g


def _mla_prep_kernel(p_ref, cos_ref, sin_ref, qn_ref, wuq_ref, kvn_ref, wukt_ref,
                     rows_ref, rowsb_ref, ct_ref, qf_ref):
    blk = p_ref[...]
    tm = blk.shape[0]
    qd = blk[:, 0:256]
    kvd = blk[:, 256:384]
    kr = blk[:, 384:416]
    q = _dot(_rms(qd, qn_ref[...]).astype(BF16), wuq_ref[...])
    cos = cos_ref[...]
    sin = sin_ref[...]
    x1 = q[:, 512:640]
    x2 = q[:, 640:768]
    r1 = x1 * cos - x2 * sin
    r2 = x2 * cos + x1 * sin
    c_kv = _rms(kvd, kvn_ref[...])
    c16 = cos[:, :16]
    s16 = sin[:, :16]
    k1 = kr[:, :16]
    k2 = kr[:, 16:32]
    k_rope = jnp.concatenate([k1 * c16 - k2 * s16, k2 * c16 + k1 * s16], axis=-1)
    rows_ref[...] = jnp.concatenate([c_kv, k_rope], axis=-1)
    pad = jnp.zeros((tm, 96), F32)
    rowsb_ref[...] = jnp.concatenate([c_kv, k_rope, pad], axis=-1).astype(BF16)
    ct_ref[...] = c_kv.T.astype(BF16)
    scale = (MLA_NOPE + MLA_ROPE) ** -0.5
    for h in range(MLA_HEADS):
        ql = _dot(q[:, h * 64:(h + 1) * 64].astype(BF16), wukt_ref[h])
        qr = jnp.concatenate([r1[:, h * 16:(h + 1) * 16], r2[:, h * 16:(h + 1) * 16]], axis=-1)
        qf_ref[h] = (jnp.concatenate([ql, qr, pad], axis=-1) * scale).astype(BF16)


def _mla_prep(proj, cos, sin, mw):
    m = proj.shape[0]
    tm = _pick(m, (256, 128))
    q_norm, wuq, kv_norm, wukt = mw
    return pl.pallas_call(
        _mla_prep_kernel,
        grid=(m // tm,),
        in_specs=[pl.BlockSpec((tm, 512), lambda i: (i, OFF_MLA // 512)),
                  pl.BlockSpec((tm, 128), lambda i: (i, 0)),
                  pl.BlockSpec((tm, 128), lambda i: (i, 0)),
                  pl.BlockSpec((1, 256), lambda i: (0, 0)),
                  pl.BlockSpec((256, 768), lambda i: (0, 0)),
                  pl.BlockSpec((1, 128), lambda i: (0, 0)),
                  pl.BlockSpec((MLA_HEADS, 64, 128), lambda i: (0, 0, 0))],
        out_specs=[pl.BlockSpec((tm, MLA_ROW), lambda i: (i, 0)),
                   pl.BlockSpec((tm, 256), lambda i: (i, 0)),
                   pl.BlockSpec((MLA_KV_RANK, tm), lambda i: (0, i)),
                   pl.BlockSpec((MLA_HEADS, tm, 256), lambda i: (0, i, 0))],
        out_shape=[jax.ShapeDtypeStruct((m, MLA_ROW), F32),
                   jax.ShapeDtypeStruct((m, 256), BF16),
                   jax.ShapeDtypeStruct((MLA_KV_RANK, m), BF16),
                   jax.ShapeDtypeStruct((MLA_HEADS, m, 256), BF16)],
        compiler_params=_cparams(("parallel",)),
        name="mla_prep",
    )(proj, cos, sin, q_norm, wuq, kv_norm, wukt)


def _mla_attn_kernel(it_ref, jt_ref, q_ref, k_ref, ct_ref, wuvt_ref, y_ref, m_ref, l_ref, acc_ref, *, tq, tk):
    t = pl.program_id(1)
    i = it_ref[t]
    j = jt_ref[t]
    j_last = (i * tq) // tk
    nh = MLA_HEADS

    @pl.when(j == 0)
    def _():
        m_ref[...] = jnp.full_like(m_ref, NEG_BIG)
        l_ref[...] = jnp.zeros_like(l_ref)
        acc_ref[...] = jnp.zeros_like(acc_ref)

    def step(causal):
        q = q_ref[...].reshape(nh * tq, 256)
        s = _dot_nt(k_ref[...], q)
        if causal:
            key = lax.broadcasted_iota(jnp.int32, s.shape, 0)
            qry = lax.broadcasted_iota(jnp.int32, s.shape, 1) % tq
            s = jnp.where(key <= qry + (i * tq - j * tk), s, NEG_BIG)
        m_old = m_ref[...]
        m_new = jnp.maximum(m_old, jnp.max(s, axis=0, keepdims=True))
        alpha = jnp.exp(m_old - m_new)
        p = jnp.exp(s - m_new)
        l_ref[...] = alpha * l_ref[...] + jnp.sum(p, axis=0, keepdims=True)
        acc_ref[...] = alpha * acc_ref[...] + _dot(ct_ref[...], p.astype(BF16))
        m_ref[...] = m_new

    @pl.when(j < j_last)
    def _():
        step(False)

    @pl.when(j == j_last)
    def _():
        step(True)
        o_t = (acc_ref[...] / l_ref[...]).astype(BF16)
        y_t = jnp.concatenate([_dot(wuvt_ref[h], o_t[:, h * tq:(h + 1) * tq]) for h in range(nh)], axis=0)
        y_ref[...] = y_t.T


def _mla_attn_prompt(qf, rowsb, ct, wuvt, nb, t):
    tq = _pick(t, (256, 128))
    nq = t // tq
    tk = 2 * tq if t % (2 * tq) == 0 else tq
    nk = t // tk
    pairs = [(i, j) for i in range(nq) for j in range((i * tq) // tk + 1)]
    i_tab = jnp.asarray([p[0] for p in pairs], jnp.int32)
    j_tab = jnp.asarray([p[1] for p in pairs], jnp.int32)
    grid_spec = pltpu.PrefetchScalarGridSpec(
        num_scalar_prefetch=2,
        grid=(nb, len(pairs)),
        in_specs=[pl.BlockSpec((MLA_HEADS, tq, 256), lambda b, t_, it, jt: (0, b * nq + it[t_], 0)),
                  pl.BlockSpec((tk, 256), lambda b, t_, it, jt: (b * nk + jt[t_], 0)),
                  pl.BlockSpec((MLA_KV_RANK, tk), lambda b, t_, it, jt: (0, b * nk + jt[t_])),
                  pl.BlockSpec((MLA_HEADS, 64, 128), lambda b, t_, it, jt: (0, 0, 0))],
        out_specs=pl.BlockSpec((tq, 512), lambda b, t_, it, jt: (b * nq + it[t_], 0)),
        scratch_shapes=[pltpu.VMEM((1, MLA_HEADS * tq), F32), pltpu.VMEM((1, MLA_HEADS * tq), F32),
                        pltpu.VMEM((MLA_KV_RANK, MLA_HEADS * tq), F32)])
    return pl.pallas_call(
        functools.partial(_mla_attn_kernel, tq=tq, tk=tk),
        grid_spec=grid_spec,
        out_shape=jax.ShapeDtypeStruct((nb * t, 512), F32),
        compiler_params=_cparams(("parallel", "arbitrary")),
        name="mla_attn_prompt",
    )(i_tab, j_tab, qf, rowsb, ct, wuvt)


def _page_copy(cache_ref, buf_ref, sem_ref, row, slot, j):
    return pltpu.make_async_copy(cache_ref.at[row], buf_ref.at[slot, j], sem_ref.at[slot])


def _stream_step(pt_ref, cache_ref, buf_ref, sem_ref, *, layer, depth, pc):
    b = pl.program_id(0)
    c = pl.program_id(1)
    nb = pl.num_programs(0)
    nc = pl.num_programs(1)
    step = b * nc + c
    slot = step % 2

    def fetch(bb, cc, sl):
        for j in range(pc):
            row = pt_ref[bb, cc * pc + j] * depth + layer
            _page_copy(cache_ref, buf_ref, sem_ref, row, sl, j).start()

    @pl.when(step == 0)
    def _():
        fetch(b, c, slot)

    @pl.when(step + 1 < nb * nc)
    def _():
        wrap = c + 1 == nc
        fetch(jnp.where(wrap, b + 1, b), jnp.where(wrap, 0, c + 1), 1 - slot)

    for j in range(pc):
        _page_copy(cache_ref, buf_ref, sem_ref, 0, slot, j).wait()
    return slot


def _mla_decode_kernel(pt_ref, q_ref, new_ref, cache_ref, o_ref, buf_ref, sem_ref, m_ref, l_ref, acc_ref,
                       *, layer, depth, pc):
    c = pl.program_id(1)
    nc = pl.num_programs(1)
    slot = _stream_step(pt_ref, cache_ref, buf_ref, sem_ref, layer=layer, depth=depth, pc=pc)

    @pl.when(c == 0)
    def _():
        m_ref[...] = jnp.full_like(m_ref, NEG_BIG)
        l_ref[...] = jnp.zeros_like(l_ref)
        acc_ref[...] = jnp.zeros_like(acc_ref)

    q = q_ref[0]
    qf = jnp.broadcast_to(q[None, :, :MLA_ROW], (pc, MLA_HEADS, MLA_ROW))
    kt = buf_ref[slot].astype(BF16)
    kc = kt[:, :MLA_KV_RANK, :]
    s = jnp.einsum('phk,pkn->phn', qf, kt, preferred_element_type=F32)
    m_old = m_ref[...]
    m_new = jnp.maximum(m_old, jnp.max(jnp.max(s, axis=0), axis=-1, keepdims=True))
    alpha = jnp.exp(m_old - m_new)
    p = jnp.exp(s - m_new[None])
    l_new = alpha * l_ref[...] + jnp.sum(jnp.sum(p, axis=0), axis=-1, keepdims=True)
    pv = jnp.einsum('phn,pcn->phc', p.astype(BF16), kc, preferred_element_type=F32)
    acc_new = alpha * acc_ref[...] + jnp.sum(pv, axis=0)
    m_ref[...] = m_new
    l_ref[...] = l_new
    acc_ref[...] = acc_new

    @pl.when(c == nc - 1)
    def _():
        new = new_ref[0].astype(F32)
        s_n = jnp.sum(q.astype(F32) * new, axis=-1, keepdims=True)
        m_f = jnp.maximum(m_new, s_n)
        al = jnp.exp(m_new - m_f)
        p_n = jnp.exp(s_n - m_f)
        l_f = al * l_new + p_n
        acc_f = al * acc_new + p_n.astype(BF16).astype(F32) * new[:, :MLA_KV_RANK]
        o_ref[0] = acc_f / l_f


def _mla_decode(page_table, qf_s, rowsb_s, cache, layer, depth):
    nb, n_pages = page_table.shape
    pc = _pick(n_pages, (128, 64, 32, 16, 8, 4, 2, 1))
    nc = n_pages // pc
    grid_spec = pltpu.PrefetchScalarGridSpec(
        num_scalar_prefetch=1,
        grid=(nb, nc),
        in_specs=[pl.BlockSpec((1, MLA_HEADS, 256), lambda b, c, pt: (b, 0, 0)),
                  pl.BlockSpec((1, 1, 256), lambda b, c, pt: (b, 0, 0)),
                  pl.BlockSpec(memory_space=pl.ANY)],
        out_specs=pl.BlockSpec((1, MLA_HEADS, MLA_KV_RANK), lambda b, c, pt: (b, 0, 0)),
        scratch_shapes=[pltpu.VMEM((2, pc, MLA_ROW, PAGE_SIZE), F32),
                        pltpu.SemaphoreType.DMA((2,)),
                        pltpu.VMEM((MLA_HEADS, 1), F32), pltpu.VMEM((MLA_HEADS, 1), F32),
                        pltpu.VMEM((MLA_HEADS, MLA_KV_RANK), F32)])
    return pl.pallas_call(
        functools.partial(_mla_decode_kernel, layer=layer, depth=depth, pc=pc),
        grid_spec=grid_spec,
        out_shape=jax.ShapeDtypeStruct((nb, MLA_HEADS, MLA_KV_RANK), F32),
        compiler_params=_cparams(("arbitrary", "arbitrary")),
        name="mla_decode",
    )(page_table, qf_s, rowsb_s.reshape(nb, 1, 256), cache)


def _mla_out_kernel(o_ref, wuv_ref, y_ref):
    for h in range(MLA_HEADS):
        y_ref[:, h * 64:(h + 1) * 64] = _dot(o_ref[h].astype(BF16), wuv_ref[h])


def _mla_out(o_hm, wuv):
    m = o_hm.shape[1]
    return pl.pallas_call(
        _mla_out_kernel,
        grid=(1,),
        in_specs=[pl.BlockSpec((MLA_HEADS, m, 128), lambda i: (0, 0, 0)),
                  pl.BlockSpec((MLA_HEADS, 128, 64), lambda i: (0, 0, 0))],
        out_specs=pl.BlockSpec((m, 512), lambda i: (0, 0)),
        out_shape=jax.ShapeDtypeStruct((m, 512), F32),
        compiler_params=_cparams(("arbitrary",)),
        name="mla_out",
    )(o_hm, wuv)


def _topk_select(gate, gate_row, n_valid, topk):
    nblk = gate.shape[0]
    blk = lax.broadcasted_iota(jnp.int32, gate.shape, 0)
    rank = jnp.zeros(gate.shape, F32)
    for m in range(nblk):
        g_m = gate_row(m)
        ahead = jnp.where(g_m > gate, 1.0, jnp.where((g_m == gate) & (m < blk), 1.0, 0.0))
        rank = rank + ahead * jnp.where(m < n_valid, 1.0, 0.0)
    return jnp.where((rank < topk) & (blk < n_valid), 1.0, 0.0)


def _moba_prompt_kernel(q_ref, kv_ref, slope_ref, y_ref, vt_ref, kb_ref, km_ref, gate_ref, sel_ref, *, nblk):
    i = pl.program_id(1)
    nq = MB_GROUP * MB_BLOCK
    scale = MB_HEAD_DIM ** -0.5

    @pl.when(i == 0)
    def _():
        for n in range(nblk):
            blk = kv_ref[n * MB_BLOCK:(n + 1) * MB_BLOCK, :]
            vt_ref[n] = blk[:, 128:256].T.astype(BF16)
            kb_ref[n * MB_BLOCK:(n + 1) * MB_BLOCK, :] = blk[:, 0:128].astype(BF16)
            km_ref[n:n + 1, :] = jnp.mean(blk[:, 0:128], axis=0, keepdims=True)

    q = q_ref[...]
    qoff = (lax.broadcasted_iota(jnp.int32, (1, nq), 1) % MB_BLOCK).astype(F32)
    koff = lax.broadcasted_iota(jnp.int32, (MB_BLOCK, 1), 0).astype(F32)
    outs = []
    qsb, slope, bias0 = [], [], []
    for g in range(MB_KV_HEADS):
        qs = jnp.concatenate([q[:, (g * MB_GROUP + e) * 64:(g * MB_GROUP + e + 1) * 64]
                              for e in range(MB_GROUP)], axis=0)
        qsb.append((qs * scale).astype(BF16))
        slope.append(slope_ref[g:g + 1, :])
        bias0.append(slope[g] * koff)
        gate = _dot_nt(km_ref[:, g * 64:(g + 1) * 64], qs, precision=HIGHEST)
        gate_ref[...] = gate
        sel_ref[g] = _topk_select(gate, lambda m: gate_ref[m:m + 1, :], i, MB_TOPK)

    def block_scores(g, n):
        kb = kb_ref[pl.ds(pl.multiple_of(n * MB_BLOCK, MB_BLOCK), MB_BLOCK), g * 64:(g + 1) * 64]
        return _dot_nt(kb, qsb[g]) + bias0[g]

    def accumulate(g, blocks, carry):
        m_old, l_old, acc = carry
        m_new = m_old
        for _, s, rb in blocks:
            m_new = jnp.maximum(m_new, jnp.max(s, axis=0, keepdims=True) + rb)
        alpha = jnp.exp(m_old - m_new)
        ps = [jnp.exp(s - (m_new - rb)) for _, s, rb in blocks]
        l_new = alpha * l_old
        for p in ps:
            l_new = l_new + jnp.sum(p, axis=0, keepdims=True)
        vt = jnp.concatenate([vt_ref[n][g * 64:(g + 1) * 64, :] for n, _, _ in blocks], axis=1)
        pcat = jnp.concatenate([p.astype(BF16) for p in ps], axis=0)
        return m_new, l_new, alpha * acc + _dot(vt, pcat)

    def row_bias(g, n):
        return slope[g] * (n * MB_BLOCK).astype(F32) + jnp.where(sel_ref[g, pl.ds(n, 1), :] > 0.5, 0.0, NEG_BIG)

    carry = []
    for g in range(MB_KV_HEADS):
        s_own = jnp.where(koff <= qoff, block_scores(g, i), NEG_BIG)
        init = (jnp.full((1, nq), NEG_BIG, F32), jnp.zeros((1, nq), F32), jnp.zeros((64, nq), F32))
        carry.append(accumulate(g, [(i, s_own, slope[g] * (i * MB_BLOCK).astype(F32))], init))

    def body(n2, carry):
        n0 = 2 * n2
        n1 = n0 + 1
        return tuple(accumulate(g, [(n0, block_scores(g, n0), row_bias(g, n0)),
                                    (n1, block_scores(g, n1), row_bias(g, n1))], carry[g])
                     for g in range(MB_KV_HEADS))

    carry = lax.fori_loop(0, (i + 1) // 2, body, tuple(carry))
    outs = [acc / l_f for _, l_f, acc in carry]

    ot = jnp.concatenate(outs, axis=0).T
    for g in range(MB_KV_HEADS):
        for e in range(MB_GROUP):
            hh = g * MB_GROUP + e
            y_ref[:, hh * 64:(hh + 1) * 64] = ot[e * MB_BLOCK:(e + 1) * MB_BLOCK, g * 64:(g + 1) * 64]


def _moba_prompt(proj, slopes_row, nb, t):
    assert t % MB_BLOCK == 0
    nblk = t // MB_BLOCK
    return pl.pallas_call(
        functools.partial(_moba_prompt_kernel, nblk=nblk),
        grid=(nb, nblk),
        in_specs=[pl.BlockSpec((MB_BLOCK, 512), lambda b, i: (b * nblk + i, OFF_MQ // 512)),
                  pl.BlockSpec((t, 256), lambda b, i: (b, OFF_KV // 256)),
                  pl.BlockSpec((MB_KV_HEADS, MB_GROUP * MB_BLOCK), lambda b, i: (0, 0))],
        out_specs=pl.BlockSpec((MB_BLOCK, 512), lambda b, i: (b * nblk + i, 0)),
        out_shape=jax.ShapeDtypeStruct((nb * t, 512), F32),
        scratch_shapes=[pltpu.VMEM((nblk, 128, MB_BLOCK), BF16), pltpu.VMEM((t, 128), BF16),
                        pltpu.VMEM((nblk, 128), F32),
                        pltpu.VMEM((nblk, MB_GROUP * MB_BLOCK), F32),
                        pltpu.VMEM((MB_KV_HEADS, nblk, MB_GROUP * MB_BLOCK), F32)],
        compiler_params=_cparams(("parallel", "arbitrary")),
        name="moba_prompt",
    )(proj, proj, slopes_row)


def _moba_keys_kernel(pt_ref, q_ref, knew_ref, slope_ref, cache_ref, p_ref, pown_ref, idx_ref,
                      buf_ref, sem_ref, s_ref, gate_ref, *, layer, depth, pc, past_len):
    slot = _stream_step(pt_ref, cache_ref, buf_ref, sem_ref, layer=layer, depth=depth, pc=pc)
    ppb = MB_BLOCK // PAGE_SIZE
    nblk = pc // ppb
    scale = MB_HEAD_DIM ** -0.5
    q = q_ref[0]
    qb = (q * scale).astype(BF16)
    q_hi, q_lo = _split_bf16(q)
    lhs = jnp.concatenate([qb, q_hi, q_lo], axis=0)
    for n in range(nblk):
        g_hi = None
        lo_sum = None
        for e in range(ppb):
            kt = buf_ref[slot, n * ppb + e]
            kt_hi = kt.astype(BF16)
            r = _dot(lhs, kt_hi)
            s_ref[n * ppb + e] = r[0:8]
            g = r[8:16] + r[16:24]
            lo = kt - kt_hi.astype(F32)
            g_hi = g if g_hi is None else g_hi + g
            lo_sum = lo if lo_sum is None else lo_sum + lo
        g3 = g_hi + _dot(q_hi, lo_sum.astype(BF16))
        gate_ref[n] = jnp.sum(g3, axis=-1, keepdims=True)
    s = s_ref[...]
    gate = gate_ref[...]

    blk_f = lax.broadcasted_iota(jnp.int32, gate.shape, 0).astype(F32)
    sel = jnp.zeros(gate.shape, F32)
    picks = []
    for _ in range(MB_TOPK):
        top = jnp.max(gate, axis=0, keepdims=True)
        pick = jnp.min(jnp.where(gate == top, blk_f, float(nblk)), axis=0, keepdims=True)
        hit = blk_f == pick
        sel = jnp.where(hit, 1.0, sel)
        gate = jnp.where(hit, NEG_BIG, gate)
        picks.append(pick)
    idx_ref[0] = jnp.concatenate(picks, axis=0).astype(jnp.int32)

    shp = (nblk, ppb, MB_HEADS, PAGE_SIZE)
    pos = (lax.broadcasted_iota(jnp.int32, shp, 0) * MB_BLOCK + lax.broadcasted_iota(jnp.int32, shp, 1) * PAGE_SIZE
           + lax.broadcasted_iota(jnp.int32, shp, 3))
    dist = (past_len - pos).astype(F32)
    slope = slope_ref[...]
    s_m = jnp.where(sel[:, None] > 0.5, s.reshape(shp) - slope[None, None] * dist, NEG_BIG)
    s_own = jnp.sum(qb.astype(F32) * knew_ref[0].astype(BF16).astype(F32), axis=-1, keepdims=True)
    m = jnp.maximum(jnp.max(jnp.max(s_m, axis=(0, 1)), axis=-1, keepdims=True), s_own)
    e = jnp.exp(s_m - m[None, None])
    e_own = jnp.exp(s_own - m)
    inv = 1.0 / (jnp.sum(jnp.sum(e, axis=(0, 1)), axis=-1, keepdims=True) + e_own)
    p_ref[0] = (e * inv[None, None]).reshape(pc, MB_HEADS, PAGE_SIZE)
    pown_ref[0] = jnp.broadcast_to(e_own * inv, (MB_HEADS, 128))


def _moba_keys(page_table, q_pad, k_new, slopes_col, cache, layer, depth):
    nb, n_pages = page_table.shape
    pc = n_pages
    past_len = n_pages * PAGE_SIZE
    assert past_len // MB_BLOCK >= MB_TOPK
    grid_spec = pltpu.PrefetchScalarGridSpec(
        num_scalar_prefetch=1,
        grid=(nb, 1),
        in_specs=[pl.BlockSpec((1, MB_HEADS, 128), lambda b, c, pt: (b, 0, 0)),
                  pl.BlockSpec((1, 1, 128), lambda b, c, pt: (b, 0, 0)),
                  pl.BlockSpec((MB_HEADS, 1), lambda b, c, pt: (0, 0)),
                  pl.BlockSpec(memory_space=pl.ANY)],
        out_specs=[pl.BlockSpec((1, pc, MB_HEADS, PAGE_SIZE), lambda b, c, pt: (b, 0, 0, 0)),
                   pl.BlockSpec((1, MB_HEADS, 128), lambda b, c, pt: (b, 0, 0)),
                   pl.BlockSpec((1, MB_TOPK, MB_HEADS, 1), lambda b, c, pt: (b, 0, 0, 0))],
        scratch_shapes=[pltpu.VMEM((2, pc, 128, PAGE_SIZE), F32),
                        pltpu.SemaphoreType.DMA((2,)),
                        pltpu.VMEM((pc, MB_HEADS, PAGE_SIZE), F32),
                        pltpu.VMEM((pc * PAGE_SIZE // MB_BLOCK, MB_HEADS, 1), F32)])
    return pl.pallas_call(
        functools.partial(_moba_keys_kernel, layer=layer, depth=depth, pc=pc, past_len=past_len),
        grid_spec=grid_spec,
        out_shape=[jax.ShapeDtypeStruct((nb, pc, MB_HEADS, PAGE_SIZE), F32),
                   jax.ShapeDtypeStruct((nb, MB_HEADS, 128), F32),
                   jax.ShapeDtypeStruct((nb, MB_TOPK, MB_HEADS, 1), jnp.int32)],
        compiler_params=_cparams(("arbitrary", "arbitrary")),
        name="moba_keys",
    )(page_table, q_pad, k_new.reshape(nb, 1, 128), slopes_col, cache)


def _moba_values_kernel(pt_ref, idx_ref, p_ref, pown_ref, vnew_ref, cache_ref, o_ref, buf_ref, sem_ref,
                        *, layer, depth):
    b = pl.program_id(0)
    nb = pl.num_programs(0)
    slot = b % 2
    ppb = MB_BLOCK // PAGE_SIZE
    n_half = MB_HEADS * MB_TOPK * ppb

    def half_page(row, h, sl, i):
        return pltpu.make_async_copy(cache_ref.at[row, pl.ds((h // MB_GROUP) * 64, 64), :],
                                     buf_ref.at[sl, i], sem_ref.at[sl])

    def fetch(bb, sl):
        for h in range(MB_HEADS):
            for j in range(MB_TOPK):
                blk = idx_ref[bb, h * MB_TOPK + j]
                for e in range(ppb):
                    row = pt_ref[bb, blk * ppb + e] * depth + layer
                    half_page(row, h, sl, (h * MB_TOPK + j) * ppb + e).start()

    @pl.when(b == 0)
    def _():
        fetch(b, slot)

    @pl.when(b + 1 < nb)
    def _():
        fetch(b + 1, 1 - slot)

    for i in range(n_half):
        half_page(0, 0, slot, i).wait()

    row_id = lax.broadcasted_iota(jnp.int32, (MB_HEADS, 1), 0)
    acc = jnp.zeros((MB_HEADS, 64), F32)
    for h in range(MB_HEADS):
        vts, prs = [], []
        for j in range(MB_TOPK):
            blk = idx_ref[b, h * MB_TOPK + j]
            for e in range(ppb):
                vts.append(buf_ref[slot, (h * MB_TOPK + j) * ppb + e])
                prs.append(p_ref[0, blk * ppb + e, h:h + 1, :])
        vt = jnp.concatenate(vts, axis=1).astype(BF16)
        pr = jnp.concatenate(prs, axis=1)
        lhs = jnp.where(row_id == h, pr, 0.0).astype(BF16)
        acc = acc + _dot_nt(lhs, vt)
    v_new = vnew_ref[0].astype(BF16).astype(F32)
    v8 = jnp.where(row_id < MB_GROUP, v_new[:, 0:64], v_new[:, 64:128])
    o_ref[0] = acc + pown_ref[0][:, 0:64].astype(BF16).astype(F32) * v8


def _moba_values(page_table, idx, p, p_own, v_new, cache, layer, depth):
    nb, n_pages = page_table.shape
    ppb = MB_BLOCK // PAGE_SIZE
    n_half = MB_HEADS * MB_TOPK * ppb
    grid_spec = pltpu.PrefetchScalarGridSpec(
        num_scalar_prefetch=2,
        grid=(nb,),
        in_specs=[pl.BlockSpec((1, n_pages, MB_HEADS, PAGE_SIZE), lambda b, pt, ix: (b, 0, 0, 0)),
                  pl.BlockSpec((1, MB_HEADS, 128), lambda b, pt, ix: (b, 0, 0)),
                  pl.BlockSpec((1, 1, 128), lambda b, pt, ix: (b, 0, 0)),
                  pl.BlockSpec(memory_space=pl.ANY)],
        out_specs=pl.BlockSpec((1, MB_HEADS, 64), lambda b, pt, ix: (b, 0, 0)),
        scratch_shapes=[pltpu.VMEM((2, n_half, 64, PAGE_SIZE), F32),
                        pltpu.SemaphoreType.DMA((2,))])
    return pl.pallas_call(
        functools.partial(_moba_values_kernel, layer=layer, depth=depth),
        grid_spec=grid_spec,
        out_shape=jax.ShapeDtypeStruct((nb, MB_HEADS, 64), F32),
        compiler_params=_cparams(("arbitrary",)),
        name="moba_values",
    )(page_table, idx, p, p_own, v_new.reshape(nb, 1, 128), cache)


def _merge_kernel(ya_ref, yb_ref, yc_ref, gate_ref, mg_ref, x_ref, wb_ref, wo_ref, g_ref, b_ref, o_ref, *, alpha):
    merged = None
    for n, y_ref in enumerate((ya_ref, yb_ref, yc_ref)):
        gcol = gate_ref[:, n * BR_WIDTH:(n + 1) * BR_WIDTH]
        o = y_ref[...] * (gcol * _sigmoid(gcol))
        br = _dot(o.astype(BF16), wb_ref[n])
        term = br * _sigmoid(mg_ref[:, n * D_MODEL:(n + 1) * D_MODEL])
        merged = term if merged is None else merged + term
    z = alpha * x_ref[...] + _dot(merged.astype(BF16), wo_ref[...])
    mu = jnp.mean(z, axis=-1, keepdims=True)
    zc = z - mu
    var = jnp.mean(zc * zc, axis=-1, keepdims=True)
    o_ref[...] = zc * lax.rsqrt(var + 1e-5) * g_ref[...] + b_ref[...]


def _merge(ya, yb, yc, proj, x, wb, wo, ln_g, ln_b, alpha):
    m = x.shape[0]
    tm = _pick(m, (256, 128))
    row = lambda w, j=0: pl.BlockSpec((tm, w), lambda i: (i, j))
    return pl.pallas_call(
        functools.partial(_merge_kernel, alpha=alpha),
        grid=(m // tm,),
        in_specs=[row(512), row(512), row(512),
                  row(1536, OFF_GATE // 1536), row(3072, OFF_MERGE // 3072), row(D_MODEL),
                  pl.BlockSpec((N_BRANCH, BR_WIDTH, D_MODEL), lambda i: (0, 0, 0)),
                  pl.BlockSpec((D_MODEL, D_MODEL), lambda i: (0, 0)),
                  pl.BlockSpec((1, D_MODEL), lambda i: (0, 0)),
                  pl.BlockSpec((1, D_MODEL), lambda i: (0, 0))],
        out_specs=row(D_MODEL),
        out_shape=jax.ShapeDtypeStruct((m, D_MODEL), F32),
        compiler_params=_cparams(("parallel",)),
        name="merge",
    )(ya, yb, yc, proj, proj, x, wb, wo, ln_g, ln_b)


def _rope_tables(pos):
    half = MLA_ROPE // 2
    inv = ROPE_THETA ** (-jnp.arange(half, dtype=F32) / half)
    ang = pos.astype(F32)[:, None] * inv[None, :]
    return jnp.tile(jnp.cos(ang), (1, MLA_HEADS)), jnp.tile(jnp.sin(ang), (1, MLA_HEADS))


def _permute_w_in(w):
    d = w.shape[0]
    z = lambda n: jnp.zeros((d, n), w.dtype)
    return jnp.concatenate([w[:, 0:1664], z(128), w[:, 2592:2848], w[:, 1664:2080], z(96),
                            w[:, 2080:2592], w[:, 4384:7456], w[:, 2848:4384]], axis=1).astype(BF16)


def kernel(x_prompt, x_sample, cache_mla, cache_moba_k, cache_moba_v, state_wkv, state_shift, page_table, w_in, rw_mu, rw_w0, rw_w2, rw_a0, rw_a2, rw_k_k, rw_k_a, rw_r_k, rw_gn_g, rw_gn_b, mla_q_norm, mla_w_uq, mla_kv_norm, mla_w_uk, mla_w_uv, w_branch, w_out, ln_g, ln_b):
    bp, tp, d = x_prompt.shape
    bs, ts, _ = x_sample.shape
    assert ts == 1 and d == D_MODEL
    depth = w_in.shape[0]
    n_pool = cache_mla.shape[0]
    n_pages = page_table.shape[1]
    past_len = n_pages * PAGE_SIZE
    assert past_len % MB_BLOCK == 0
    alpha = (2 * depth) ** 0.25

    cache_mla2 = jnp.transpose(cache_mla, (0, 1, 3, 2)).reshape(n_pool * depth, MLA_ROW, PAGE_SIZE)
    cache_k2 = jnp.transpose(cache_moba_k, (0, 1, 3, 4, 2)).reshape(n_pool * depth, 128, PAGE_SIZE)
    cache_v2 = jnp.transpose(cache_moba_v, (0, 1, 3, 4, 2)).reshape(n_pool * depth, 128, PAGE_SIZE)

    cos_p, sin_p = _rope_tables(jnp.tile(jnp.arange(tp), bp))
    cos_s, sin_s = _rope_tables(jnp.full((bs,), past_len))
    slopes = 2.0 ** (-8.0 * jnp.arange(1, MB_HEADS + 1, dtype=F32) / MB_HEADS)
    slopes_row = jnp.repeat(slopes.reshape(MB_KV_HEADS, MB_GROUP), MB_BLOCK, axis=1)
    slopes_col = slopes.reshape(MB_HEADS, 1)
    half = MLA_ROPE // 2
    hd = MLA_NOPE + MLA_ROPE

    xp = x_prompt.reshape(bp * tp, d)
    xs = x_sample.reshape(bs, d)
    zero_shift = jnp.zeros((bp, RW_SHIFT_W), F32)
    zero_wkv = jnp.zeros((bp, RW_HEADS, 64, 64), F32)

    mla_p, mla_s, kp, ksm, vp, vsm, wp, wsm, sp, ssm = [], [], [], [], [], [], [], [], [], []
    for l in range(depth):
        w_perm = _permute_w_in(w_in[l])
        r2 = lambda a: a.reshape(1, -1)
        rw = (r2(rw_mu[l]), r2(rw_w0[l]), rw_w2[l].astype(BF16), r2(rw_a0[l]), rw_a2[l].astype(BF16),
              r2(rw_k_k[l]), r2(rw_k_a[l]), r2(rw_r_k[l]), r2(rw_gn_g[l]), r2(rw_gn_b[l]))
        wuq3 = mla_w_uq[l].reshape(MLA_Q_RANK, MLA_HEADS, hd)
        wuq = jnp.concatenate([wuq3[:, :, :MLA_NOPE].reshape(MLA_Q_RANK, -1),
                               wuq3[:, :, MLA_NOPE:MLA_NOPE + half].reshape(MLA_Q_RANK, -1),
                               wuq3[:, :, MLA_NOPE + half:].reshape(MLA_Q_RANK, -1)], axis=1).astype(BF16)
        wukt = jnp.transpose(mla_w_uk[l], (1, 2, 0)).astype(BF16)
        wuv = jnp.transpose(mla_w_uv[l], (1, 0, 2)).astype(BF16)
        wuvt = jnp.transpose(mla_w_uv[l], (1, 2, 0)).astype(BF16)
        mw = (r2(mla_q_norm[l]), wuq, r2(mla_kv_norm[l]), wukt)
        wb = w_branch[l].astype(BF16)
        wo = w_out[l].astype(BF16)
        lg, lb = r2(ln_g[l]), r2(ln_b[l])

        proj = _inproj(xp, w_perm)
        ya, wkv_new = _rwkv_prompt(proj, zero_shift, zero_wkv, rw, bp, tp)
        rows, rowsb, ct, qf = _mla_prep(proj, cos_p, sin_p, mw)
        yb = _mla_attn_prompt(qf, rowsb, ct, wuvt, bp, tp)
        yc = _moba_prompt(proj, slopes_row, bp, tp)
        xp_new = _merge(ya, yb, yc, proj, xp, wb, wo, lg, lb, alpha)
        mla_p.append(rows.reshape(bp, tp, MLA_ROW))
        kp.append(proj[:, OFF_KV:OFF_KV + 128].reshape(bp, tp, MB_KV_HEADS, MB_HEAD_DIM))
        vp.append(proj[:, OFF_KV + 128:OFF_KV + 256].reshape(bp, tp, MB_KV_HEADS, MB_HEAD_DIM))
        wp.append(wkv_new)
        sp.append(proj.reshape(bp, tp, PROJ_W)[:, tp - 1, :RW_SHIFT_W])
        xp = xp_new

        proj = _inproj(xs, w_perm)
        ya, wkv_t = _rwkv_sample(proj, state_shift[l], jnp.transpose(state_wkv[l], (1, 2, 3, 0)), rw)
        wkv_new = jnp.transpose(wkv_t, (3, 0, 1, 2))
        rows, rowsb, _, qf = _mla_prep(proj, cos_s, sin_s, mw)
        o_lat = _mla_decode(page_table, jnp.transpose(qf, (1, 0, 2)), rowsb, cache_mla2, l, depth)
        yb = _mla_out(jnp.transpose(o_lat, (1, 0, 2)), wuv)
        k_new = proj[:, OFF_KV:OFF_KV + 128]
        v_new = proj[:, OFF_KV + 128:OFF_KV + 256]
        q4 = proj[:, OFF_MQ:OFF_MQ + 512].reshape(bs, MB_KV_HEADS, MB_GROUP, MB_HEAD_DIM)
        q_pad = jnp.concatenate(
            [jnp.pad(q4[:, g], ((0, 0), (0, 0), (g * 64, (MB_KV_HEADS - 1 - g) * 64))) for g in range(MB_KV_HEADS)],
            axis=1)
        p_att, p_own, idx = _moba_keys(page_table, q_pad, k_new, slopes_col, cache_k2, l, depth)
        idx = jnp.transpose(idx[..., 0], (0, 2, 1)).reshape(bs, MB_HEADS * MB_TOPK)
        yc = _moba_values(page_table, idx, p_att, p_own, v_new, cache_v2, l, depth).reshape(bs, MB_HEADS * 64)
        xs_new = _merge(ya, yb, yc, proj, xs, wb, wo, lg, lb, alpha)
        mla_s.append(rows.reshape(bs, 1, MLA_ROW))
        ksm.append(k_new.reshape(bs, 1, MB_KV_HEADS, MB_HEAD_DIM))
        vsm.append(v_new.reshape(bs, 1, MB_KV_HEADS, MB_HEAD_DIM))
        wsm.append(wkv_new)
        ssm.append(proj[:, :RW_SHIFT_W])
        xs = xs_new

    return (xp.reshape(bp, tp, d), xs.reshape(bs, 1, d),
            jnp.stack(mla_p, 1), jnp.stack(mla_s, 1),
            jnp.stack(kp, 1), jnp.stack(ksm, 1), jnp.stack(vp, 1), jnp.stack(vsm, 1),
            jnp.stack(wp, 0), jnp.stack(wsm, 0), jnp.stack(sp, 0), jnp.stack(ssm, 0))
```

```python
import functools
import math

import jax
import jax.numpy as jnp
from jax import lax
from jax.experimental import pallas as pl
from jax.experimental.pallas import tpu as pltpu

F32 = jnp.float32
BF16 = jnp.bfloat16
HIGHEST = lax.Precision.HIGHEST

D_MODEL = 1024
PAGE_SIZE = 128
RW_HEADS = 8
RW_HEAD_DIM = 64
RW_WIDTH = 512
RW_DECAY_RANK = 64
RW_SHIFT_W = 1664
RW_GN_EPS = 64e-5
MLA_HEADS = 8
MLA_NOPE = 64
MLA_ROPE = 32
MLA_V = 64
MLA_Q_RANK = 256
MLA_KV_RANK = 128
MLA_ROW = 160
ROPE_THETA = 10000.0
MB_HEADS = 8
MB_KV_HEADS = 2
MB_HEAD_DIM = 64
MB_GROUP = 4
MB_BLOCK = 256
MB_TOPK = 3
N_BRANCH = 3
BR_WIDTH = 512
NEG_BIG = -1e30

OFF_RW = 0
OFF_KV = 1792
OFF_MLA = 2048
OFF_MQ = 2560
OFF_MERGE = 3072
OFF_GATE = 6144
PROJ_W = 7680

VMEM_LIMIT = 56 * 1024 * 1024


def _cparams(sem, vmem=VMEM_LIMIT):
    return pltpu.CompilerParams(dimension_semantics=sem, vmem_limit_bytes=vmem)


def _pick(n, prefs):
    for p in prefs:
        if n % p == 0:
            return p
    return n


def _dot(a, b, **kw):
    return jnp.dot(a, b, preferred_element_type=F32, **kw)


def _dot_nt(a, b, **kw):
    return lax.dot_general(a, b, (((1,), (1,)), ((), ())), preferred_element_type=F32, **kw)


def _split_bf16(x):
    hi = x.astype(BF16)
    lo = (x - hi.astype(F32)).astype(BF16)
    return hi, lo


def _mm_any(f, a, b, passes):
    if passes == 1:
        return f(a.astype(BF16), b.astype(BF16))
    a_hi, a_lo = _split_bf16(a)
    b_hi, b_lo = _split_bf16(b)
    return f(a_hi, b_hi) + f(a_hi, b_lo) + f(a_lo, b_hi)


def _mm(a, b, passes):
    return _mm_any(_dot, a, b, passes)


def _mm_nt(a, b, passes):
    return _mm_any(_dot_nt, a, b, passes)


def _bmm(a, b, passes):
    return _mm_any(lambda x, y: jnp.einsum('hij,hjk->hik', x, y, preferred_element_type=F32), a, b, passes)


def _bmm_nt(a, b, passes):
    return _mm_any(lambda x, y: jnp.einsum('hik,hjk->hij', x, y, preferred_element_type=F32), a, b, passes)


RW_PASSES_A = 1
RW_PASSES_INV = 1


def _sigmoid(x):
    return 1.0 / (1.0 + jnp.exp(-x))


def _inproj_kernel(x_ref, w_ref, o_ref, xb_ref):
    @pl.when(pl.program_id(1) == 0)
    def _():
        xb_ref[...] = x_ref[...].astype(BF16)

    o_ref[...] = _dot(xb_ref[...], w_ref[...])


def _inproj(x, w_perm):
    m, d = x.shape
    tm = _pick(m, (1024, 512, 256, 128))
    tn = 1280
    return pl.pallas_call(
        _inproj_kernel,
        grid=(m // tm, PROJ_W // tn),
        in_specs=[pl.BlockSpec((tm, d), lambda i, j: (i, 0)),
                  pl.BlockSpec((d, tn), lambda i, j: (0, j))],
        out_specs=pl.BlockSpec((tm, tn), lambda i, j: (i, j)),
        out_shape=jax.ShapeDtypeStruct((m, PROJ_W), F32),
        scratch_shapes=[pltpu.VMEM((tm, d), BF16)],
        compiler_params=_cparams(("parallel", "arbitrary")),
        name="inproj",
    )(x, w_perm)


def _rwkv_prep(cols, prev, mu, w0, w2b, a0, a2b, k_k, k_a):
    mixed = cols + (prev - cols) * mu
    r = mixed[:, 0:512]
    k = mixed[:, 512:1024]
    v = mixed[:, 1024:1536]
    wd = mixed[:, 1536:1600]
    ad = mixed[:, 1600:1664]
    zw = w0 + _dot(jnp.tanh(wd).astype(BF16), w2b)
    nz = -zw
    softplus = jnp.maximum(nz, 0.0) + jnp.log(1.0 + jnp.exp(-jnp.abs(nz)))
    lw = -jnp.exp(-softplus - 0.5)
    a = _sigmoid(a0 + _dot(ad.astype(BF16), a2b))
    kk = k * k_k
    k_mod = k * (1.0 + (a - 1.0) * k_a)
    return r, k_mod, v, lw, a, kk


def _head_norm(kk_h):
    ss = jnp.sum(kk_h * kk_h, axis=-1, keepdims=True)
    return kk_h * lax.rsqrt(jnp.maximum(ss, 1e-24))


def _group_norm_bonus(y_h, r_h, kmod_h, v_h, rk_h, g_h, b_h):
    ym = jnp.mean(y_h, axis=-1, keepdims=True)
    yc = y_h - ym
    yv = jnp.mean(yc * yc, axis=-1, keepdims=True)
    yn = yc * lax.rsqrt(yv + RW_GN_EPS) * g_h + b_h
    bonus = jnp.sum(r_h * kmod_h * rk_h, axis=-1, keepdims=True) * v_h
    return yn + bonus


def _rwkv_chunk_kernel(p_ref, shift0_ref, wkv0_ref, mu_ref, w0_ref, w2_ref, a0_ref, a2_ref,
                       kk_ref, ka_ref, rk_ref, gg_ref, gb_ref,
                       y_ref, wkv_ref, carry_ref, s_ref, *, chunk):
    c = pl.program_id(1)
    nc = pl.num_programs(1)
    C = chunk

    @pl.when(c == 0)
    def _():
        carry_ref[...] = shift0_ref[0]
        s_ref[...] = wkv0_ref[0]

    cols = p_ref[...]
    rows = lax.broadcasted_iota(jnp.int32, cols.shape, 0)
    prev = jnp.where(rows == 0, carry_ref[...], pltpu.roll(cols, 1, 0))
    carry_ref[...] = cols[C - 1:C, :]

    r, k_mod, v, lw, a, kk = _rwkv_prep(cols, prev, mu_ref[...], w0_ref[...], w2_ref[...],
                                        a0_ref[...], a2_ref[...], kk_ref[...], ka_ref[...])

    ti = lax.broadcasted_iota(jnp.int32, (C, C), 0)
    si = lax.broadcasted_iota(jnp.int32, (C, C), 1)
    lower = ti >= si
    strict = ti > si
    eye = (ti == si).astype(F32)
    tri = lower.astype(BF16)
    lw_hi = lw.astype(BF16)
    lw_r = lw - lw_hi.astype(F32)
    lw_mid = lw_r.astype(BF16)
    lw_lo = (lw_r - lw_mid.astype(F32)).astype(BF16)
    cum = _dot(tri, lw_hi) + _dot(tri, lw_mid) + _dot(tri, lw_lo)
    p_incl = jnp.exp(cum)
    p_excl = jnp.exp(cum - lw)
    p_inv = jnp.exp(-cum)
    p_end = p_incl[C - 1:C, :]

    base = 16
    diag_mask = (ti // base) == (si // base)
    off_masks = []
    size = base
    while size < C:
        off_masks.append(((ti // (2 * size)) == (si // (2 * size))) & ((ti // size) != (si // size)))
        size *= 2
    heads = lambda x: jnp.stack([x[:, h * 64:(h + 1) * 64] for h in range(RW_HEADS)], axis=0)
    kk3 = heads(kk)
    kk3 = kk3 * lax.rsqrt(jnp.maximum(jnp.sum(kk3 * kk3, axis=-1, keepdims=True), 1e-24))
    v3, r3, km3 = heads(v), heads(r), heads(k_mod)
    pinv3 = heads(p_inv)
    at = -(kk3 * heads(p_excl))
    bt = kk3 * heads(a) * pinv3
    kt = km3 * pinv3
    rt = r3 * heads(p_incl)
    s0 = s_ref[...]

    lhs = jnp.concatenate([at, rt], axis=1)
    rhs = jnp.concatenate([bt, kt], axis=1)
    mx = _bmm_nt(lhs, rhs, RW_PASSES_A)
    a_ab = jnp.where(strict[None], mx[:, :C, :C], 0.0)
    a_ak = jnp.where(strict[None], mx[:, :C, C:], 0.0)
    r_b = jnp.where(lower[None], mx[:, C:, :C], 0.0)
    r_k = jnp.where(lower[None], mx[:, C:, C:], 0.0)

    npow = jnp.where(diag_mask[None], a_ab, 0.0)
    inv = eye[None] + npow
    for _ in range(3):
        npow = _bmm(npow, npow, RW_PASSES_INV)
        inv = inv + _bmm(npow, inv, RW_PASSES_INV)
    for off_mask in off_masks:
        inv = inv + _bmm(inv, _bmm(jnp.where(off_mask[None], a_ab, 0.0), inv, RW_PASSES_INV), RW_PASSES_INV)

    sa = _bmm_nt(lhs, s0, RW_PASSES_A)
    rhs_u = sa[:, :C] + _bmm(a_ak, v3, RW_PASSES_A)
    u = _bmm(inv, rhs_u, RW_PASSES_INV)
    y3 = sa[:, C:] + _bmm(r_b, u, RW_PASSES_A) + _bmm(r_k, v3, RW_PASSES_A)

    for h in range(RW_HEADS):
        sl = slice(h * 64, (h + 1) * 64)
        pe = p_end[:, sl]
        uv_t = jnp.concatenate([u[h], v3[h]], axis=1).T
        bk = jnp.concatenate([bt[h] * pe, kt[h] * pe], axis=1)
        upd = _mm(uv_t, bk, RW_PASSES_A)
        s_ref[h] = s0[h] * pe + upd[:64, :64] + upd[64:, 64:]
        y_ref[:, sl] = _group_norm_bonus(y3[h], r3[h], km3[h], v3[h], rk_ref[:, sl], gg_ref[:, sl], gb_ref[:, sl])

    @pl.when(c == nc - 1)
    def _():
        wkv_ref[0] = s_ref[...]


def _rwkv_prompt(proj, shift0, wkv0, rw, nb, t):
    chunk = 128
    assert t % chunk == 0
    nc = t // chunk
    vec = lambda n: pl.BlockSpec((1, n), lambda b, c: (0, 0))
    mat = lambda a, b_: pl.BlockSpec((a, b_), lambda b, c: (0, 0))
    return pl.pallas_call(
        functools.partial(_rwkv_chunk_kernel, chunk=chunk),
        grid=(nb, nc),
        in_specs=[pl.BlockSpec((chunk, RW_SHIFT_W), lambda b, c: (b * nc + c, 0)),
                  pl.BlockSpec((1, 1, RW_SHIFT_W), lambda b, c: (b, 0, 0)),
                  pl.BlockSpec((1, RW_HEADS, 64, 64), lambda b, c: (b, 0, 0, 0)),
                  vec(RW_SHIFT_W), vec(512), mat(64, 512), vec(512), mat(64, 512),
                  vec(512), vec(512), vec(512), vec(512), vec(512)],
        out_specs=[pl.BlockSpec((chunk, 512), lambda b, c: (b * nc + c, 0)),
                   pl.BlockSpec((1, RW_HEADS, 64, 64), lambda b, c: (b, 0, 0, 0))],
        out_shape=[jax.ShapeDtypeStruct((nb * t, 512), F32),
                   jax.ShapeDtypeStruct((nb, RW_HEADS, 64, 64), F32)],
        scratch_shapes=[pltpu.VMEM((1, RW_SHIFT_W), F32), pltpu.VMEM((RW_HEADS, 64, 64), F32)],
        compiler_params=_cparams(("parallel", "arbitrary")),
        name="rwkv_prompt",
    )(proj, shift0.reshape(nb, 1, RW_SHIFT_W), wkv0, *rw)


def _rwkv_step_kernel(p_ref, shift0_ref, wkv0_ref, mu_ref, w0_ref, w2_ref, a0_ref, a2_ref,
                      kk_ref, ka_ref, rk_ref, gg_ref, gb_ref,
                      y_ref, wkv_ref, r_s, k_s, v_s, wt_s, kt_s, kkt_s, bt_s, rt_s, vt_s, yt_s):
    h = pl.program_id(0)
    nh = pl.num_programs(0)

    @pl.when(h == 0)
    def _():
        r, k_mod, v, lw, a, kk = _rwkv_prep(p_ref[:, :RW_SHIFT_W], shift0_ref[...], mu_ref[...], w0_ref[...],
                                            w2_ref[...], a0_ref[...], a2_ref[...], kk_ref[...], ka_ref[...])
        kkn = jnp.concatenate([_head_norm(kk[:, g * 64:(g + 1) * 64]) for g in range(RW_HEADS)], axis=1)
        r_s[...] = r
        k_s[...] = k_mod
        v_s[...] = v
        wt_s[...] = jnp.exp(lw).T
        kt_s[...] = k_mod.T
        kkt_s[...] = kkn.T
        bt_s[...] = (kkn * a).T
        rt_s[...] = r.T
        vt_s[...] = v.T

    base = pl.multiple_of(h * 64, 64)
    w_t = wt_s[pl.ds(base, 64), :]
    k_t = kt_s[pl.ds(base, 64), :]
    kk_t = kkt_s[pl.ds(base, 64), :]
    b_t = bt_s[pl.ds(base, 64), :]
    r_t = rt_s[pl.ds(base, 64), :]

    def body(vi, carry):
        s0 = wkv0_ref[0, vi]
        sa = -jnp.sum(s0 * kk_t, axis=0, keepdims=True)
        s1 = s0 * w_t + sa * b_t + vt_s[pl.ds(base + vi, 1), :] * k_t
        wkv_ref[0, vi] = s1
        yt_s[pl.ds(base + vi, 1), :] = jnp.sum(s1 * r_t, axis=0, keepdims=True)
        return carry

    lax.fori_loop(0, RW_HEAD_DIM, body, 0)

    @pl.when(h == nh - 1)
    def _():
        y = yt_s[...].T
        for g in range(RW_HEADS):
            sl = slice(g * 64, (g + 1) * 64)
            y_ref[:, sl] = _group_norm_bonus(y[:, sl], r_s[:, sl], k_s[:, sl], v_s[:, sl],
                                             rk_ref[:, sl], gg_ref[:, sl], gb_ref[:, sl])


def _rwkv_sample(proj, shift0, wkv0_t, rw):
    nb = proj.shape[0]
    vec = lambda n: pl.BlockSpec((1, n), lambda i: (0, 0))
    mat = lambda a, b_: pl.BlockSpec((a, b_), lambda i: (0, 0))
    sq = pltpu.VMEM((nb, 512), F32)
    st = pltpu.VMEM((512, nb), F32)
    return pl.pallas_call(
        _rwkv_step_kernel,
        grid=(RW_HEADS,),
        in_specs=[pl.BlockSpec((nb, 1792), lambda i: (0, 0)),
                  mat(nb, RW_SHIFT_W),
                  pl.BlockSpec((1, 64, 64, nb), lambda i: (i, 0, 0, 0)),
                  vec(RW_SHIFT_W), vec(512), mat(64, 512), vec(512), mat(64, 512),
                  vec(512), vec(512), vec(512), vec(512), vec(512)],
        out_specs=[pl.BlockSpec((nb, 512), lambda i: (0, 0)),
                   pl.BlockSpec((1, 64, 64, nb), lambda i: (i, 0, 0, 0))],
        out_shape=[jax.ShapeDtypeStruct((nb, 512), F32),
                   jax.ShapeDtypeStruct((RW_HEADS, 64, 64, nb), F32)],
        scratch_shapes=[sq, sq, sq, st, st, st, st, st, st, st],
        compiler_params=_cparams(("arbitrary",)),
        name="rwkv_sample",
    )(proj, shift0, wkv0_t, *rw)


def _rms(x, g):
    return x * lax.rsqrt(jnp.mean(x * x, axis=-1, keepdims=True) + 1e-6) * g


def _mla_prep_kernel(p_ref, cos_ref, sin_ref, qn_ref, wuq_ref, kvn_ref, wukt_ref,
                     rows_ref, rowsb_ref, ct_ref, qf_ref):
    blk = p_ref[...]
    tm = blk.shape[0]
    qd = blk[:, 0:256]
    kvd = blk[:, 256:384]
    kr = blk[:, 384:416]
    q = _dot(_rms(qd, qn_ref[...]).astype(BF16), wuq_ref[...])
    cos = cos_ref[...]
    sin = sin_ref[...]
    x1 = q[:, 512:640]
    x2 = q[:, 640:768]
    r1 = x1 * cos - x2 * sin
    r2 = x2 * cos + x1 * sin
    c_kv = _rms(kvd, kvn_ref[...])
    c16 = cos[:, :16]
    s16 = sin[:, :16]
    k1 = kr[:, :16]
    k2 = kr[:, 16:32]
    k_rope = jnp.concatenate([k1 * c16 - k2 * s16, k2 * c16 + k1 * s16], axis=-1)
    rows_ref[...] = jnp.concatenate([c_kv, k_rope], axis=-1)
    pad = jnp.zeros((tm, 96), F32)
    rowsb_ref[...] = jnp.concatenate([c_kv, k_rope, pad], axis=-1).astype(BF16)
    ct_ref[...] = c_kv.T.astype(BF16)
    scale = (MLA_NOPE + MLA_ROPE) ** -0.5
    for h in range(MLA_HEADS):
        ql = _dot(q[:, h * 64:(h + 1) * 64].astype(BF16), wukt_ref[h])
        qr = jnp.concatenate([r1[:, h * 16:(h + 1) * 16], r2[:, h * 16:(h + 1) * 16]], axis=-1)
        qf_ref[h] = (jnp.concatenate([ql, qr, pad], axis=-1) * scale).astype(BF16)


def _mla_prep(proj, cos, sin, mw):
    m = proj.shape[0]
    tm = _pick(m, (256, 128))
    q_norm, wuq, kv_norm, wukt = mw
    return pl.pallas_call(
        _mla_prep_kernel,
        grid=(m // tm,),
        in_specs=[pl.BlockSpec((tm, 512), lambda i: (i, OFF_MLA // 512)),
                  pl.BlockSpec((tm, 128), lambda i: (i, 0)),
                  pl.BlockSpec((tm, 128), lambda i: (i, 0)),
                  pl.BlockSpec((1, 256), lambda i: (0, 0)),
                  pl.BlockSpec((256, 768), lambda i: (0, 0)),
                  pl.BlockSpec((1, 128), lambda i: (0, 0)),
                  pl.BlockSpec((MLA_HEADS, 64, 128), lambda i: (0, 0, 0))],
        out_specs=[pl.BlockSpec((tm, MLA_ROW), lambda i: (i, 0)),
                   pl.BlockSpec((tm, 256), lambda i: (i, 0)),
                   pl.BlockSpec((MLA_KV_RANK, tm), lambda i: (0, i)),
                   pl.BlockSpec((MLA_HEADS, tm, 256), lambda i: (0, i, 0))],
        out_shape=[jax.ShapeDtypeStruct((m, MLA_ROW), F32),
                   jax.ShapeDtypeStruct((m, 256), BF16),
                   jax.ShapeDtypeStruct((MLA_KV_RANK, m), BF16),
                   jax.ShapeDtypeStruct((MLA_HEADS, m, 256), BF16)],
        compiler_params=_cparams(("parallel",)),
        name="mla_prep",
    )(proj, cos, sin, q_norm, wuq, kv_norm, wukt)


def _mla_attn_kernel(it_ref, jt_ref, q_ref, k_ref, ct_ref, wuvt_ref, y_ref, m_ref, l_ref, acc_ref, *, tq, tk):
    t = pl.program_id(1)
    i = it_ref[t]
    j = jt_ref[t]
    j_last = (i * tq) // tk
    nh = MLA_HEADS

    @pl.when(j == 0)
    def _():
        m_ref[...] = jnp.full_like(m_ref, NEG_BIG)
        l_ref[...] = jnp.zeros_like(l_ref)
        acc_ref[...] = jnp.zeros_like(acc_ref)

    def step(causal):
        q = q_ref[...].reshape(nh * tq, 256)
        s = _dot_nt(k_ref[...], q)
        if causal:
            key = lax.broadcasted_iota(jnp.int32, s.shape, 0)
            qry = lax.broadcasted_iota(jnp.int32, s.shape, 1) % tq
            s = jnp.where(key <= qry + (i * tq - j * tk), s, NEG_BIG)
        m_old = m_ref[...]
        m_new = jnp.maximum(m_old, jnp.max(s, axis=0, keepdims=True))
        alpha = jnp.exp(m_old - m_new)
        p = jnp.exp(s - m_new)
        l_ref[...] = alpha * l_ref[...] + jnp.sum(p, axis=0, keepdims=True)
        acc_ref[...] = alpha * acc_ref[...] + _dot(ct_ref[...], p.astype(BF16))
        m_ref[...] = m_new

    @pl.when(j < j_last)
    def _():
        step(False)

    @pl.when(j == j_last)
    def _():
        step(True)
        o_t = (acc_ref[...] / l_ref[...]).astype(BF16)
        y_t = jnp.concatenate([_dot(wuvt_ref[h], o_t[:, h * tq:(h + 1) * tq]) for h in range(nh)], axis=0)
        y_ref[...] = y_t.T


def _mla_attn_prompt(qf, rowsb, ct, wuvt, nb, t):
    tq = _pick(t, (256, 128))
    nq = t // tq
    tk = 2 * tq if t % (2 * tq) == 0 else tq
    nk = t // tk
    pairs = [(i, j) for i in range(nq) for j in range((i * tq) // tk + 1)]
    i_tab = jnp.asarray([p[0] for p in pairs], jnp.int32)
    j_tab = jnp.asarray([p[1] for p in pairs], jnp.int32)
    grid_spec = pltpu.PrefetchScalarGridSpec(
        num_scalar_prefetch=2,
        grid=(nb, len(pairs)),
        in_specs=[pl.BlockSpec((MLA_HEADS, tq, 256), lambda b, t_, it, jt: (0, b * nq + it[t_], 0)),
                  pl.BlockSpec((tk, 256), lambda b, t_, it, jt: (b * nk + jt[t_], 0)),
                  pl.BlockSpec((MLA_KV_RANK, tk), lambda b, t_, it, jt: (0, b * nk + jt[t_])),
                  pl.BlockSpec((MLA_HEADS, 64, 128), lambda b, t_, it, jt: (0, 0, 0))],
        out_specs=pl.BlockSpec((tq, 512), lambda b, t_, it, jt: (b * nq + it[t_], 0)),
        scratch_shapes=[pltpu.VMEM((1, MLA_HEADS * tq), F32), pltpu.VMEM((1, MLA_HEADS * tq), F32),
                        pltpu.VMEM((MLA_KV_RANK, MLA_HEADS * tq), F32)])
    return pl.pallas_call(
        functools.partial(_mla_attn_kernel, tq=tq, tk=tk),
        grid_spec=grid_spec,
        out_shape=jax.ShapeDtypeStruct((nb * t, 512), F32),
        compiler_params=_cparams(("parallel", "arbitrary")),
        name="mla_attn_prompt",
    )(i_tab, j_tab, qf, rowsb, ct, wuvt)


def _page_copy(cache_ref, buf_ref, sem_ref, row, slot, j):
    return pltpu.make_async_copy(cache_ref.at[row], buf_ref.at[slot, j], sem_ref.at[slot])


def _stream_step(rows_ref, cache_ref, buf_ref, sem_ref, *, pc):
    b = pl.program_id(0)
    c = pl.program_id(1)
    nc = pl.num_programs(1)
    step = b * nc + c
    last = pl.num_programs(0) * nc - 1
    slot = step % 2
    nxt = jnp.minimum(step + 1, last)
    nb_, nc_ = nxt // nc, nxt % nc

    @pl.when(step == 0)
    def _():
        for j in range(pc):
            _page_copy(cache_ref, buf_ref, sem_ref, rows_ref[b, c * pc + j], slot, j).start()

    for j in range(pc):
        _page_copy(cache_ref, buf_ref, sem_ref, 0, slot, j).wait()

    def prefetch(j):
        _page_copy(cache_ref, buf_ref, sem_ref, rows_ref[nb_, nc_ * pc + j], 1 - slot, j).start()

    def drain():
        @pl.when(step == last)
        def _():
            for j in range(pc):
                _page_copy(cache_ref, buf_ref, sem_ref, 0, 1 - slot, j).wait()

    return slot, prefetch, drain


def _mla_decode_kernel(pt_ref, q_ref, new_ref, cache_ref, o_ref, buf_ref, sem_ref, m_ref, l_ref, acc_ref,
                       *, pc):
    c = pl.program_id(1)
    nc = pl.num_programs(1)
    slot, prefetch, drain = _stream_step(pt_ref, cache_ref, buf_ref, sem_ref, pc=pc)

    @pl.when(c == 0)
    def _():
        m_ref[...] = jnp.full_like(m_ref, NEG_BIG)
        l_ref[...] = jnp.zeros_like(l_ref)
        acc_ref[...] = jnp.zeros_like(acc_ref)

    q = q_ref[0]
    grp = _pick(pc, (16, 8, 4, 2, 1))
    qf = jnp.broadcast_to(q[None, :, :MLA_ROW], (grp, MLA_HEADS, MLA_ROW))
    kts, ss = [], []
    for g0 in range(0, pc, grp):
        for j in range(g0, g0 + grp):
            prefetch(j)
        kt_g = buf_ref[slot, g0:g0 + grp].astype(BF16)
        kts.append(kt_g)
        ss.append(jnp.einsum('phk,pkn->phn', qf, kt_g, preferred_element_type=F32))
    kt = jnp.concatenate(kts, axis=0)
    kc = kt[:, :MLA_KV_RANK, :]
    s = jnp.concatenate(ss, axis=0)
    m_old = m_ref[...]
    m_new = jnp.maximum(m_old, jnp.max(jnp.max(s, axis=0), axis=-1, keepdims=True))
    alpha = jnp.exp(m_old - m_new)
    p = jnp.exp(s - m_new[None])
    l_new = alpha * l_ref[...] + jnp.sum(jnp.sum(p, axis=0), axis=-1, keepdims=True)
    pv = jnp.einsum('phn,pcn->phc', p.astype(BF16), kc, preferred_element_type=F32)
    acc_new = alpha * acc_ref[...] + jnp.sum(pv, axis=0)
    m_ref[...] = m_new
    l_ref[...] = l_new
    acc_ref[...] = acc_new

    @pl.when(c == nc - 1)
    def _():
        new = new_ref[0].astype(F32)
        s_n = jnp.sum(q.astype(F32) * new, axis=-1, keepdims=True)
        m_f = jnp.maximum(m_new, s_n)
        al = jnp.exp(m_new - m_f)
        p_n = jnp.exp(s_n - m_f)
        l_f = al * l_new + p_n
        acc_f = al * acc_new + p_n.astype(BF16).astype(F32) * new[:, :MLA_KV_RANK]
        o_ref[0] = acc_f / l_f

    drain()


def _mla_decode(page_table, qf_s, rowsb_s, cache):
    nb, n_pages = page_table.shape
    pc = _pick(n_pages, (128, 64, 32, 16, 8, 4, 2, 1))
    nc = n_pages // pc
    grid_spec = pltpu.PrefetchScalarGridSpec(
        num_scalar_prefetch=1,
        grid=(nb, nc),
        in_specs=[pl.BlockSpec((1, MLA_HEADS, 256), lambda b, c, pt: (b, 0, 0)),
                  pl.BlockSpec((1, 1, 256), lambda b, c, pt: (b, 0, 0)),
                  pl.BlockSpec(memory_space=pl.ANY)],
        out_specs=pl.BlockSpec((1, MLA_HEADS, MLA_KV_RANK), lambda b, c, pt: (b, 0, 0)),
        scratch_shapes=[pltpu.VMEM((2, pc, MLA_ROW, PAGE_SIZE), F32),
                        pltpu.SemaphoreType.DMA((2,)),
                        pltpu.VMEM((MLA_HEADS, 1), F32), pltpu.VMEM((MLA_HEADS, 1), F32),
                        pltpu.VMEM((MLA_HEADS, MLA_KV_RANK), F32)])
    return pl.pallas_call(
        functools.partial(_mla_decode_kernel, pc=pc),
        grid_spec=grid_spec,
        out_shape=jax.ShapeDtypeStruct((nb, MLA_HEADS, MLA_KV_RANK), F32),
        compiler_params=_cparams(("arbitrary", "arbitrary")),
        name="mla_decode",
    )(page_table, qf_s, rowsb_s.reshape(nb, 1, 256), cache)


def _mla_out_kernel(o_ref, wuv_ref, y_ref):
    for h in range(MLA_HEADS):
        y_ref[:, h * 64:(h + 1) * 64] = _dot(o_ref[h].astype(BF16), wuv_ref[h])


def _mla_out(o_hm, wuv):
    m = o_hm.shape[1]
    return pl.pallas_call(
        _mla_out_kernel,
        grid=(1,),
        in_specs=[pl.BlockSpec((MLA_HEADS, m, 128), lambda i: (0, 0, 0)),
                  pl.BlockSpec((MLA_HEADS, 128, 64), lambda i: (0, 0, 0))],
        out_specs=pl.BlockSpec((m, 512), lambda i: (0, 0)),
        out_shape=jax.ShapeDtypeStruct((m, 512), F32),
        compiler_params=_cparams(("arbitrary",)),
        name="mla_out",
    )(o_hm, wuv)


def _topk_select(gate, gate_row, n_valid, topk):
    nblk = gate.shape[0]
    blk = lax.broadcasted_iota(jnp.int32, gate.shape, 0)
    rank = jnp.zeros(gate.shape, F32)
    for m in range(nblk):
        g_m = gate_row(m)
        ahead = jnp.where(g_m > gate, 1.0, jnp.where((g_m == gate) & (m < blk), 1.0, 0.0))
        rank = rank + ahead * jnp.where(m < n_valid, 1.0, 0.0)
    return jnp.where((rank < topk) & (blk < n_valid), 1.0, 0.0)


def _moba_prompt_kernel(q_ref, kv_ref, slope_ref, y_ref, vt_ref, kb_ref, km_ref, gate_ref, sel_ref, *, nblk):
    i = pl.program_id(1)
    nq = MB_GROUP * MB_BLOCK
    scale = MB_HEAD_DIM ** -0.5

    @pl.when(i == 0)
    def _():
        for n in range(nblk):
            blk = kv_ref[n * MB_BLOCK:(n + 1) * MB_BLOCK, :]
            vt_ref[n] = blk[:, 128:256].T.astype(BF16)
            kb_ref[n * MB_BLOCK:(n + 1) * MB_BLOCK, :] = blk[:, 0:128].astype(BF16)
            km_ref[n:n + 1, :] = jnp.mean(blk[:, 0:128], axis=0, keepdims=True)

    q = q_ref[...]
    qoff = (lax.broadcasted_iota(jnp.int32, (1, nq), 1) % MB_BLOCK).astype(F32)
    koff = lax.broadcasted_iota(jnp.int32, (MB_BLOCK, 1), 0).astype(F32)
    outs = []
    qsb, slope, bias0 = [], [], []
    for g in range(MB_KV_HEADS):
        qs = jnp.concatenate([q[:, (g * MB_GROUP + e) * 64:(g * MB_GROUP + e + 1) * 64]
                              for e in range(MB_GROUP)], axis=0)
        qsb.append((qs * scale).astype(BF16))
        slope.append(slope_ref[g:g + 1, :])
        bias0.append(slope[g] * koff)
        gate = _dot_nt(km_ref[:, g * 64:(g + 1) * 64], qs, precision=HIGHEST)
        gate_ref[...] = gate
        sel_ref[g] = _topk_select(gate, lambda m: gate_ref[m:m + 1, :], i, MB_TOPK)

    def block_scores(g, n):
        kb = kb_ref[pl.ds(pl.multiple_of(n * MB_BLOCK, MB_BLOCK), MB_BLOCK), g * 64:(g + 1) * 64]
        return _dot_nt(kb, qsb[g]) + bias0[g]

    def accumulate(g, blocks, carry):
        m_old, l_old, acc = carry
        m_new = m_old
        for _, s, rb in blocks:
            m_new = jnp.maximum(m_new, jnp.max(s, axis=0, keepdims=True) + rb)
        alpha = jnp.exp(m_old - m_new)
        ps = [jnp.exp(s - (m_new - rb)) for _, s, rb in blocks]
        l_new = alpha * l_old
        for p in ps:
            l_new = l_new + jnp.sum(p, axis=0, keepdims=True)
        vt = jnp.concatenate([vt_ref[n][g * 64:(g + 1) * 64, :] for n, _, _ in blocks], axis=1)
        pcat = jnp.concatenate([p.astype(BF16) for p in ps], axis=0)
        return m_new, l_new, alpha * acc + _dot(vt, pcat)

    def row_bias(g, n):
        return slope[g] * (n * MB_BLOCK).astype(F32) + jnp.where(sel_ref[g, pl.ds(n, 1), :] > 0.5, 0.0, NEG_BIG)

    carry = []
    for g in range(MB_KV_HEADS):
        s_own = jnp.where(koff <= qoff, block_scores(g, i), NEG_BIG)
        init = (jnp.full((1, nq), NEG_BIG, F32), jnp.zeros((1, nq), F32), jnp.zeros((64, nq), F32))
        carry.append(accumulate(g, [(i, s_own, slope[g] * (i * MB_BLOCK).astype(F32))], init))

    def body(n2, carry):
        n0 = 2 * n2
        n1 = n0 + 1
        return tuple(accumulate(g, [(n0, block_scores(g, n0), row_bias(g, n0)),
                                    (n1, block_scores(g, n1), row_bias(g, n1))], carry[g])
                     for g in range(MB_KV_HEADS))

    carry = lax.fori_loop(0, (i + 1) // 2, body, tuple(carry))
    outs = [acc / l_f for _, l_f, acc in carry]

    ot = jnp.concatenate(outs, axis=0).T
    for g in range(MB_KV_HEADS):
        for e in range(MB_GROUP):
            hh = g * MB_GROUP + e
            y_ref[:, hh * 64:(hh + 1) * 64] = ot[e * MB_BLOCK:(e + 1) * MB_BLOCK, g * 64:(g + 1) * 64]


def _moba_prompt(proj, slopes_row, nb, t):
    assert t % MB_BLOCK == 0
    nblk = t // MB_BLOCK
    return pl.pallas_call(
        functools.partial(_moba_prompt_kernel, nblk=nblk),
        grid=(nb, nblk),
        in_specs=[pl.BlockSpec((MB_BLOCK, 512), lambda b, i: (b * nblk + i, OFF_MQ // 512)),
                  pl.BlockSpec((t, 256), lambda b, i: (b, OFF_KV // 256)),
                  pl.BlockSpec((MB_KV_HEADS, MB_GROUP * MB_BLOCK), lambda b, i: (0, 0))],
        out_specs=pl.BlockSpec((MB_BLOCK, 512), lambda b, i: (b * nblk + i, 0)),
        out_shape=jax.ShapeDtypeStruct((nb * t, 512), F32),
        scratch_shapes=[pltpu.VMEM((nblk, 128, MB_BLOCK), BF16), pltpu.VMEM((t, 128), BF16),
                        pltpu.VMEM((nblk, 128), F32),
                        pltpu.VMEM((nblk, MB_GROUP * MB_BLOCK), F32),
                        pltpu.VMEM((MB_KV_HEADS, nblk, MB_GROUP * MB_BLOCK), F32)],
        compiler_params=_cparams(("parallel", "arbitrary")),
        name="moba_prompt",
    )(proj, proj, slopes_row)


def _moba_keys_kernel(pt_ref, q_ref, knew_ref, slope_ref, cache_ref, p_ref, pown_ref, idx_ref,
                      buf_ref, sem_ref, s_ref, gate_ref, *, pc, past_len):
    slot, prefetch, drain = _stream_step(pt_ref, cache_ref, buf_ref, sem_ref, pc=pc)
    ppb = MB_BLOCK // PAGE_SIZE
    nblk = pc // ppb
    scale = MB_HEAD_DIM ** -0.5
    q = q_ref[0]
    qb = (q * scale).astype(BF16)
    q_hi, q_lo = _split_bf16(q)
    lhs = jnp.concatenate([qb, q_hi, q_lo], axis=0)
    for n in range(nblk):
        g_hi = None
        lo_sum = None
        for e in range(ppb):
            prefetch(n * ppb + e)
            kt = buf_ref[slot, n * ppb + e]
            kt_hi = kt.astype(BF16)
            r = _dot(lhs, kt_hi)
            s_ref[n * ppb + e] = r[0:8]
            g = r[8:16] + r[16:24]
            lo = kt - kt_hi.astype(F32)
            g_hi = g if g_hi is None else g_hi + g
            lo_sum = lo if lo_sum is None else lo_sum + lo
        g3 = g_hi + _dot(q_hi, lo_sum.astype(BF16))
        gate_ref[n] = jnp.sum(g3, axis=-1, keepdims=True)
    s = s_ref[...]
    gate = gate_ref[...]

    blk_f = lax.broadcasted_iota(jnp.int32, gate.shape, 0).astype(F32)
    sel = jnp.zeros(gate.shape, F32)
    picks = []
    for _ in range(MB_TOPK):
        top = jnp.max(gate, axis=0, keepdims=True)
        pick = jnp.min(jnp.where(gate == top, blk_f, float(nblk)), axis=0, keepdims=True)
        hit = blk_f == pick
        sel = jnp.where(hit, 1.0, sel)
        gate = jnp.where(hit, NEG_BIG, gate)
        picks.append(pick)
    idx_ref[0] = jnp.concatenate(picks, axis=0).astype(jnp.int32)

    shp = (nblk, ppb, MB_HEADS, PAGE_SIZE)
    pos = (lax.broadcasted_iota(jnp.int32, shp, 0) * MB_BLOCK + lax.broadcasted_iota(jnp.int32, shp, 1) * PAGE_SIZE
           + lax.broadcasted_iota(jnp.int32, shp, 3))
    dist = (past_len - pos).astype(F32)
    slope = slope_ref[...]
    s_m = jnp.where(sel[:, None] > 0.5, s.reshape(shp) - slope[None, None] * dist, NEG_BIG)
    s_own = jnp.sum(qb.astype(F32) * knew_ref[0].astype(BF16).astype(F32), axis=-1, keepdims=True)
    m = jnp.maximum(jnp.max(jnp.max(s_m, axis=(0, 1)), axis=-1, keepdims=True), s_own)
    e = jnp.exp(s_m - m[None, None])
    e_own = jnp.exp(s_own - m)
    inv = 1.0 / (jnp.sum(jnp.sum(e, axis=(0, 1)), axis=-1, keepdims=True) + e_own)
    p_ref[0] = (e * inv[None, None]).reshape(pc, MB_HEADS, PAGE_SIZE)
    pown_ref[0] = jnp.broadcast_to(e_own * inv, (MB_HEADS, 128))
    drain()


def _moba_keys(page_table, q_pad, k_new, slopes_col, cache):
    nb, n_pages = page_table.shape
    pc = n_pages
    past_len = n_pages * PAGE_SIZE
    assert past_len // MB_BLOCK >= MB_TOPK
    grid_spec = pltpu.PrefetchScalarGridSpec(
        num_scalar_prefetch=1,
        grid=(nb, 1),
        in_specs=[pl.BlockSpec((1, MB_HEADS, 128), lambda b, c, pt: (b, 0, 0)),
                  pl.BlockSpec((1, 1, 128), lambda b, c, pt: (b, 0, 0)),
                  pl.BlockSpec((MB_HEADS, 1), lambda b, c, pt: (0, 0)),
                  pl.BlockSpec(memory_space=pl.ANY)],
        out_specs=[pl.BlockSpec((1, pc, MB_HEADS, PAGE_SIZE), lambda b, c, pt: (b, 0, 0, 0)),
                   pl.BlockSpec((1, MB_HEADS, 128), lambda b, c, pt: (b, 0, 0)),
                   pl.BlockSpec((1, MB_TOPK, MB_HEADS, 1), lambda b, c, pt: (b, 0, 0, 0))],
        scratch_shapes=[pltpu.VMEM((2, pc, 128, PAGE_SIZE), F32),
                        pltpu.SemaphoreType.DMA((2,)),
                        pltpu.VMEM((pc, MB_HEADS, PAGE_SIZE), F32),
                        pltpu.VMEM((pc * PAGE_SIZE // MB_BLOCK, MB_HEADS, 1), F32)])
    return pl.pallas_call(
        functools.partial(_moba_keys_kernel, pc=pc, past_len=past_len),
        grid_spec=grid_spec,
        out_shape=[jax.ShapeDtypeStruct((nb, pc, MB_HEADS, PAGE_SIZE), F32),
                   jax.ShapeDtypeStruct((nb, MB_HEADS, 128), F32),
                   jax.ShapeDtypeStruct((nb, MB_TOPK, MB_HEADS, 1), jnp.int32)],
        compiler_params=_cparams(("arbitrary", "arbitrary")),
        name="moba_keys",
    )(page_table, q_pad, k_new.reshape(nb, 1, 128), slopes_col, cache)


def _moba_values_kernel(pt_ref, idx_ref, p_ref, pown_ref, vnew_ref, cache_ref, o_ref, buf_ref, sem_ref):
    b = pl.program_id(0)
    nb = pl.num_programs(0)
    slot = b % 2
    ppb = MB_BLOCK // PAGE_SIZE
    n_half = MB_HEADS * MB_TOPK * ppb

    def half_page(row, h, sl, i):
        return pltpu.make_async_copy(cache_ref.at[row, pl.ds((h // MB_GROUP) * 64, 64), :],
                                     buf_ref.at[sl, i], sem_ref.at[sl])

    def fetch(bb, sl):
        for h in range(MB_HEADS):
            for j in range(MB_TOPK):
                blk = idx_ref[bb, h * MB_TOPK + j]
                for e in range(ppb):
                    half_page(pt_ref[bb, blk * ppb + e], h, sl, (h * MB_TOPK + j) * ppb + e).start()

    @pl.when(b == 0)
    def _():
        fetch(b, slot)

    @pl.when(b + 1 < nb)
    def _():
        fetch(b + 1, 1 - slot)

    for i in range(n_half):
        half_page(0, 0, slot, i).wait()

    row_id = lax.broadcasted_iota(jnp.int32, (MB_HEADS, 1), 0)
    acc = jnp.zeros((MB_HEADS, 64), F32)
    for h in range(MB_HEADS):
        vts, prs = [], []
        for j in range(MB_TOPK):
            blk = idx_ref[b, h * MB_TOPK + j]
            for e in range(ppb):
                vts.append(buf_ref[slot, (h * MB_TOPK + j) * ppb + e])
                prs.append(p_ref[0, blk * ppb + e, h:h + 1, :])
        vt = jnp.concatenate(vts, axis=1).astype(BF16)
        pr = jnp.concatenate(prs, axis=1)
        lhs = jnp.where(row_id == h, pr, 0.0).astype(BF16)
        acc = acc + _dot_nt(lhs, vt)
    v_new = vnew_ref[0].astype(BF16).astype(F32)
    v8 = jnp.where(row_id < MB_GROUP, v_new[:, 0:64], v_new[:, 64:128])
    o_ref[0] = acc + pown_ref[0][:, 0:64].astype(BF16).astype(F32) * v8


def _moba_values(page_table, idx, p, p_own, v_new, cache):
    nb, n_pages = page_table.shape
    ppb = MB_BLOCK // PAGE_SIZE
    n_half = MB_HEADS * MB_TOPK * ppb
    grid_spec = pltpu.PrefetchScalarGridSpec(
        num_scalar_prefetch=2,
        grid=(nb,),
        in_specs=[pl.BlockSpec((1, n_pages, MB_HEADS, PAGE_SIZE), lambda b, pt, ix: (b, 0, 0, 0)),
                  pl.BlockSpec((1, MB_HEADS, 128), lambda b, pt, ix: (b, 0, 0)),
                  pl.BlockSpec((1, 1, 128), lambda b, pt, ix: (b, 0, 0)),
                  pl.BlockSpec(memory_space=pl.ANY)],
        out_specs=pl.BlockSpec((1, MB_HEADS, 64), lambda b, pt, ix: (b, 0, 0)),
        scratch_shapes=[pltpu.VMEM((2, n_half, 64, PAGE_SIZE), F32),
                        pltpu.SemaphoreType.DMA((2,))])
    return pl.pallas_call(
        _moba_values_kernel,
        grid_spec=grid_spec,
        out_shape=jax.ShapeDtypeStruct((nb, MB_HEADS, 64), F32),
        compiler_params=_cparams(("arbitrary",)),
        name="moba_values",
    )(page_table, idx, p, p_own, v_new.reshape(nb, 1, 128), cache)


def _merge_kernel(ya_ref, yb_ref, yc_ref, gate_ref, mg_ref, x_ref, wb_ref, wo_ref, g_ref, b_ref, o_ref, *, alpha):
    merged = None
    for n, y_ref in enumerate((ya_ref, yb_ref, yc_ref)):
        gcol = gate_ref[:, n * BR_WIDTH:(n + 1) * BR_WIDTH]
        o = y_ref[...] * (gcol * _sigmoid(gcol))
        br = _dot(o.astype(BF16), wb_ref[n])
        term = br * _sigmoid(mg_ref[:, n * D_MODEL:(n + 1) * D_MODEL])
        merged = term if merged is None else merged + term
    z = alpha * x_ref[...] + _dot(merged.astype(BF16), wo_ref[...])
    mu = jnp.mean(z, axis=-1, keepdims=True)
    zc = z - mu
    var = jnp.mean(zc * zc, axis=-1, keepdims=True)
    o_ref[...] = zc * lax.rsqrt(var + 1e-5) * g_ref[...] + b_ref[...]


def _merge(ya, yb, yc, proj, x, wb, wo, ln_g, ln_b, alpha):
    m = x.shape[0]
    tm = _pick(m, (256, 128))
    row = lambda w, j=0: pl.BlockSpec((tm, w), lambda i: (i, j))
    return pl.pallas_call(
        functools.partial(_merge_kernel, alpha=alpha),
        grid=(m // tm,),
        in_specs=[row(512), row(512), row(512),
                  row(1536, OFF_GATE // 1536), row(3072, OFF_MERGE // 3072), row(D_MODEL),
                  pl.BlockSpec((N_BRANCH, BR_WIDTH, D_MODEL), lambda i: (0, 0, 0)),
                  pl.BlockSpec((D_MODEL, D_MODEL), lambda i: (0, 0)),
                  pl.BlockSpec((1, D_MODEL), lambda i: (0, 0)),
                  pl.BlockSpec((1, D_MODEL), lambda i: (0, 0))],
        out_specs=row(D_MODEL),
        out_shape=jax.ShapeDtypeStruct((m, D_MODEL), F32),
        compiler_params=_cparams(("parallel",)),
        name="merge",
    )(ya, yb, yc, proj, proj, x, wb, wo, ln_g, ln_b)


def _rope_tables(pos):
    half = MLA_ROPE // 2
    inv = ROPE_THETA ** (-jnp.arange(half, dtype=F32) / half)
    ang = pos.astype(F32)[:, None] * inv[None, :]
    return jnp.tile(jnp.cos(ang), (1, MLA_HEADS)), jnp.tile(jnp.sin(ang), (1, MLA_HEADS))


def _permute_w_in(w):
    d = w.shape[0]
    z = lambda n: jnp.zeros((d, n), w.dtype)
    return jnp.concatenate([w[:, 0:1664], z(128), w[:, 2592:2848], w[:, 1664:2080], z(96),
                            w[:, 2080:2592], w[:, 4384:7456], w[:, 2848:4384]], axis=1).astype(BF16)


def kernel(x_prompt, x_sample, cache_mla, cache_moba_k, cache_moba_v, state_wkv, state_shift, page_table, w_in, rw_mu, rw_w0, rw_w2, rw_a0, rw_a2, rw_k_k, rw_k_a, rw_r_k, rw_gn_g, rw_gn_b, mla_q_norm, mla_w_uq, mla_kv_norm, mla_w_uk, mla_w_uv, w_branch, w_out, ln_g, ln_b):
    bp, tp, d = x_prompt.shape
    bs, ts, _ = x_sample.shape
    assert ts == 1 and d == D_MODEL
    depth = w_in.shape[0]
    n_pool = cache_mla.shape[0]
    n_pages = page_table.shape[1]
    past_len = n_pages * PAGE_SIZE
    assert past_len % MB_BLOCK == 0
    alpha = (2 * depth) ** 0.25

    cache_mla2 = jnp.transpose(cache_mla, (0, 1, 3, 2)).reshape(n_pool * depth, MLA_ROW, PAGE_SIZE)
    cache_k2 = jnp.transpose(cache_moba_k, (0, 1, 3, 4, 2)).reshape(n_pool * depth, 128, PAGE_SIZE)
    cache_v2 = jnp.transpose(cache_moba_v, (0, 1, 3, 4, 2)).reshape(n_pool * depth, 128, PAGE_SIZE)

    cos_p, sin_p = _rope_tables(jnp.tile(jnp.arange(tp), bp))
    cos_s, sin_s = _rope_tables(jnp.full((bs,), past_len))
    slopes = 2.0 ** (-8.0 * jnp.arange(1, MB_HEADS + 1, dtype=F32) / MB_HEADS)
    slopes_row = jnp.repeat(slopes.reshape(MB_KV_HEADS, MB_GROUP), MB_BLOCK, axis=1)
    slopes_col = slopes.reshape(MB_HEADS, 1)
    half = MLA_ROPE // 2
    hd = MLA_NOPE + MLA_ROPE

    xp = x_prompt.reshape(bp * tp, d)
    xs = x_sample.reshape(bs, d)
    zero_shift = jnp.zeros((bp, RW_SHIFT_W), F32)
    zero_wkv = jnp.zeros((bp, RW_HEADS, 64, 64), F32)

    mla_p, mla_s, kp, ksm, vp, vsm, wp, wsm, sp, ssm = [], [], [], [], [], [], [], [], [], []
    for l in range(depth):
        w_perm = _permute_w_in(w_in[l])
        r2 = lambda a: a.reshape(1, -1)
        rw = (r2(rw_mu[l]), r2(rw_w0[l]), rw_w2[l].astype(BF16), r2(rw_a0[l]), rw_a2[l].astype(BF16),
              r2(rw_k_k[l]), r2(rw_k_a[l]), r2(rw_r_k[l]), r2(rw_gn_g[l]), r2(rw_gn_b[l]))
        wuq3 = mla_w_uq[l].reshape(MLA_Q_RANK, MLA_HEADS, hd)
        wuq = jnp.concatenate([wuq3[:, :, :MLA_NOPE].reshape(MLA_Q_RANK, -1),
                               wuq3[:, :, MLA_NOPE:MLA_NOPE + half].reshape(MLA_Q_RANK, -1),
                               wuq3[:, :, MLA_NOPE + half:].reshape(MLA_Q_RANK, -1)], axis=1).astype(BF16)
        wukt = jnp.transpose(mla_w_uk[l], (1, 2, 0)).astype(BF16)
        wuv = jnp.transpose(mla_w_uv[l], (1, 0, 2)).astype(BF16)
        wuvt = jnp.transpose(mla_w_uv[l], (1, 2, 0)).astype(BF16)
        mw = (r2(mla_q_norm[l]), wuq, r2(mla_kv_norm[l]), wukt)
        wb = w_branch[l].astype(BF16)
        wo = w_out[l].astype(BF16)
        lg, lb = r2(ln_g[l]), r2(ln_b[l])

        proj = _inproj(xp, w_perm)
        ya, wkv_new = _rwkv_prompt(proj, zero_shift, zero_wkv, rw, bp, tp)
        rows, rowsb, ct, qf = _mla_prep(proj, cos_p, sin_p, mw)
        yb = _mla_attn_prompt(qf, rowsb, ct, wuvt, bp, tp)
        yc = _moba_prompt(proj, slopes_row, bp, tp)
        xp_new = _merge(ya, yb, yc, proj, xp, wb, wo, lg, lb, alpha)
        mla_p.append(rows.reshape(bp, tp, MLA_ROW))
        kp.append(proj[:, OFF_KV:OFF_KV + 128].reshape(bp, tp, MB_KV_HEADS, MB_HEAD_DIM))
        vp.append(proj[:, OFF_KV + 128:OFF_KV + 256].reshape(bp, tp, MB_KV_HEADS, MB_HEAD_DIM))
        wp.append(wkv_new)
        sp.append(proj.reshape(bp, tp, PROJ_W)[:, tp - 1, :RW_SHIFT_W])
        xp = xp_new

        proj = _inproj(xs, w_perm)
        ya, wkv_t = _rwkv_sample(proj, state_shift[l], jnp.transpose(state_wkv[l], (1, 2, 3, 0)), rw)
        wkv_new = jnp.transpose(wkv_t, (3, 0, 1, 2))
        rows, rowsb, _, qf = _mla_prep(proj, cos_s, sin_s, mw)
        page_rows = page_table * depth + l
        o_lat = _mla_decode(page_rows, jnp.transpose(qf, (1, 0, 2)), rowsb, cache_mla2)
        yb = _mla_out(jnp.transpose(o_lat, (1, 0, 2)), wuv)
        k_new = proj[:, OFF_KV:OFF_KV + 128]
        v_new = proj[:, OFF_KV + 128:OFF_KV + 256]
        q4 = proj[:, OFF_MQ:OFF_MQ + 512].reshape(bs, MB_KV_HEADS, MB_GROUP, MB_HEAD_DIM)
        q_pad = jnp.concatenate(
            [jnp.pad(q4[:, g], ((0, 0), (0, 0), (g * 64, (MB_KV_HEADS - 1 - g) * 64))) for g in range(MB_KV_HEADS)],
            axis=1)
        p_att, p_own, idx = _moba_keys(page_rows, q_pad, k_new, slopes_col, cache_k2)
        idx = jnp.transpose(idx[..., 0], (0, 2, 1)).reshape(bs, MB_HEADS * MB_TOPK)
        yc = _moba_values(page_rows, idx, p_att, p_own, v_new, cache_v2).reshape(bs, MB_HEADS * 64)
        xs_new = _merge(ya, yb, yc, proj, xs, wb, wo, lg, lb, alpha)
        mla_s.append(rows.reshape(bs, 1, MLA_ROW))
        ksm.append(k_new.reshape(bs, 1, MB_KV_HEADS, MB_HEAD_DIM))
        vsm.append(v_new.reshape(bs, 1, MB_KV_HEADS, MB_HEAD_DIM))
        wsm.append(wkv_new)
        ssm.append(proj[:, :RW_SHIFT_W])
        xs = xs_new

    return (xp.reshape(bp, tp, d), xs.reshape(bs, 1, d),
            jnp.stack(mla_p, 1), jnp.stack(mla_s, 1),
            jnp.stack(kp, 1), jnp.stack(ksm, 1), jnp.stack(vp, 1), jnp.stack(vsm, 1),
            jnp.stack(wp, 0), jnp.stack(wsm, 0), jnp.stack(sp, 0), jnp.stack(ssm, 0))
```

```python
import functools
import math

import jax
import jax.numpy as jnp
from jax import lax
from jax.experimental import pallas as pl
from jax.experimental.pallas import tpu as pltpu

F32 = jnp.float32
BF16 = jnp.bfloat16
HIGHEST = lax.Precision.HIGHEST

D_MODEL = 1024
PAGE_SIZE = 128
RW_HEADS = 8
RW_HEAD_DIM = 64
RW_WIDTH = 512
RW_DECAY_RANK = 64
RW_SHIFT_W = 1664
RW_GN_EPS = 64e-5
MLA_HEADS = 8
MLA_NOPE = 64
MLA_ROPE = 32
MLA_V = 64
MLA_Q_RANK = 256
MLA_KV_RANK = 128
MLA_ROW = 160
ROPE_THETA = 10000.0
MB_HEADS = 8
MB_KV_HEADS = 2
MB_HEAD_DIM = 64
MB_GROUP = 4
MB_BLOCK = 256
MB_TOPK = 3
N_BRANCH = 3
BR_WIDTH = 512
NEG_BIG = -1e30

OFF_RW = 0
OFF_KV = 1792
OFF_MLA = 2048
OFF_MQ = 2560
OFF_MERGE = 3072
OFF_GATE = 6144
PROJ_W = 7680

VMEM_LIMIT = 56 * 1024 * 1024


def _cparams(sem, vmem=VMEM_LIMIT):
    return pltpu.CompilerParams(dimension_semantics=sem, vmem_limit_bytes=vmem)


def _pick(n, prefs):
    for p in prefs:
        if n % p == 0:
            return p
    return n


def _dot(a, b, **kw):
    return jnp.dot(a, b, preferred_element_type=F32, **kw)


def _dot_nt(a, b, **kw):
    return lax.dot_general(a, b, (((1,), (1,)), ((), ())), preferred_element_type=F32, **kw)


def _split_bf16(x):
    hi = x.astype(BF16)
    lo = (x - hi.astype(F32)).astype(BF16)
    return hi, lo


def _mm_any(f, a, b, passes):
    if passes == 1:
        return f(a.astype(BF16), b.astype(BF16))
    a_hi, a_lo = _split_bf16(a)
    b_hi, b_lo = _split_bf16(b)
    return f(a_hi, b_hi) + f(a_hi, b_lo) + f(a_lo, b_hi)


def _mm(a, b, passes):
    return _mm_any(_dot, a, b, passes)


def _mm_nt(a, b, passes):
    return _mm_any(_dot_nt, a, b, passes)


def _bmm(a, b, passes):
    return _mm_any(lambda x, y: jnp.einsum('hij,hjk->hik', x, y, preferred_element_type=F32), a, b, passes)


def _bmm_nt(a, b, passes):
    return _mm_any(lambda x, y: jnp.einsum('hik,hjk->hij', x, y, preferred_element_type=F32), a, b, passes)


RW_PASSES_A = 1
RW_PASSES_INV = 1


def _sigmoid(x):
    return 1.0 / (1.0 + jnp.exp(-x))


def _inproj_kernel(x_ref, w_ref, o_ref, xb_ref):
    @pl.when(pl.program_id(1) == 0)
    def _():
        xb_ref[...] = x_ref[...].astype(BF16)

    o_ref[...] = _dot(xb_ref[...], w_ref[...])


def _inproj(x, w_perm):
    m, d = x.shape
    tm = _pick(m, (1024, 512, 256, 128))
    tn = 2560
    return pl.pallas_call(
        _inproj_kernel,
        grid=(m // tm, PROJ_W // tn),
        in_specs=[pl.BlockSpec((tm, d), lambda i, j: (i, 0)),
                  pl.BlockSpec((d, tn), lambda i, j: (0, j))],
        out_specs=pl.BlockSpec((tm, tn), lambda i, j: (i, j)),
        out_shape=jax.ShapeDtypeStruct((m, PROJ_W), F32),
        scratch_shapes=[pltpu.VMEM((tm, d), BF16)],
        compiler_params=_cparams(("parallel", "arbitrary")),
        name="inproj",
    )(x, w_perm)


def _rwkv_prep(cols, prev, mu, w0, w2b, a0, a2b, k_k, k_a):
    mixed = cols + (prev - cols) * mu
    r = mixed[:, 0:512]
    k = mixed[:, 512:1024]
    v = mixed[:, 1024:1536]
    wd = mixed[:, 1536:1600]
    ad = mixed[:, 1600:1664]
    zw = w0 + _dot(jnp.tanh(wd).astype(BF16), w2b)
    nz = -zw
    softplus = jnp.maximum(nz, 0.0) + jnp.log(1.0 + jnp.exp(-jnp.abs(nz)))
    lw = -jnp.exp(-softplus - 0.5)
    a = _sigmoid(a0 + _dot(ad.astype(BF16), a2b))
    kk = k * k_k
    k_mod = k * (1.0 + (a - 1.0) * k_a)
    return r, k_mod, v, lw, a, kk


def _head_norm(kk_h):
    ss = jnp.sum(kk_h * kk_h, axis=-1, keepdims=True)
    return kk_h * lax.rsqrt(jnp.maximum(ss, 1e-24))


def _group_norm_bonus(y_h, r_h, kmod_h, v_h, rk_h, g_h, b_h):
    ym = jnp.mean(y_h, axis=-1, keepdims=True)
    yc = y_h - ym
    yv = jnp.mean(yc * yc, axis=-1, keepdims=True)
    yn = yc * lax.rsqrt(yv + RW_GN_EPS) * g_h + b_h
    bonus = jnp.sum(r_h * kmod_h * rk_h, axis=-1, keepdims=True) * v_h
    return yn + bonus


def _rwkv_chunk_kernel(p_ref, shift0_ref, wkv0_ref, mu_ref, w0_ref, w2_ref, a0_ref, a2_ref,
                       kk_ref, ka_ref, rk_ref, gg_ref, gb_ref,
                       y_ref, wkv_ref, carry_ref, s_ref, *, chunk):
    c = pl.program_id(1)
    nc = pl.num_programs(1)
    C = chunk

    @pl.when(c == 0)
    def _():
        carry_ref[...] = shift0_ref[0]
        s_ref[...] = wkv0_ref[0]

    cols = p_ref[...]
    rows = lax.broadcasted_iota(jnp.int32, cols.shape, 0)
    prev = jnp.where(rows == 0, carry_ref[...], pltpu.roll(cols, 1, 0))
    carry_ref[...] = cols[C - 1:C, :]

    r, k_mod, v, lw, a, kk = _rwkv_prep(cols, prev, mu_ref[...], w0_ref[...], w2_ref[...],
                                        a0_ref[...], a2_ref[...], kk_ref[...], ka_ref[...])

    ti = lax.broadcasted_iota(jnp.int32, (C, C), 0)
    si = lax.broadcasted_iota(jnp.int32, (C, C), 1)
    lower = ti >= si
    strict = ti > si
    eye = (ti == si).astype(F32)
    tri = lower.astype(BF16)
    lw_hi = lw.astype(BF16)
    lw_r = lw - lw_hi.astype(F32)
    lw_mid = lw_r.astype(BF16)
    lw_lo = (lw_r - lw_mid.astype(F32)).astype(BF16)
    cum = _dot(tri, lw_hi) + _dot(tri, lw_mid) + _dot(tri, lw_lo)
    p_incl = jnp.exp(cum)
    p_excl = jnp.exp(cum - lw)
    p_inv = jnp.exp(-cum)
    p_end = p_incl[C - 1:C, :]

    base = 16
    diag_mask = (ti // base) == (si // base)
    off_masks = []
    size = base
    while size < C:
        off_masks.append(((ti // (2 * size)) == (si // (2 * size))) & ((ti // size) != (si // size)))
        size *= 2
    heads = lambda x: jnp.stack([x[:, h * 64:(h + 1) * 64] for h in range(RW_HEADS)], axis=0)
    kk3 = heads(kk)
    kk3 = kk3 * lax.rsqrt(jnp.maximum(jnp.sum(kk3 * kk3, axis=-1, keepdims=True), 1e-24))
    v3, r3, km3 = heads(v), heads(r), heads(k_mod)
    pinv3 = heads(p_inv)
    at = -(kk3 * heads(p_excl))
    bt = kk3 * heads(a) * pinv3
    kt = km3 * pinv3
    rt = r3 * heads(p_incl)
    s0 = s_ref[...]

    lhs = jnp.concatenate([at, rt], axis=1)
    rhs = jnp.concatenate([bt, kt], axis=1)
    mx = _bmm_nt(lhs, rhs, RW_PASSES_A)
    a_ab = jnp.where(strict[None], mx[:, :C, :C], 0.0)
    a_ak = jnp.where(strict[None], mx[:, :C, C:], 0.0)
    r_b = jnp.where(lower[None], mx[:, C:, :C], 0.0)
    r_k = jnp.where(lower[None], mx[:, C:, C:], 0.0)

    npow = jnp.where(diag_mask[None], a_ab, 0.0)
    inv = eye[None] + npow
    for _ in range(3):
        npow = _bmm(npow, npow, RW_PASSES_INV)
        inv = inv + _bmm(npow, inv, RW_PASSES_INV)
    for off_mask in off_masks:
        inv = inv + _bmm(inv, _bmm(jnp.where(off_mask[None], a_ab, 0.0), inv, RW_PASSES_INV), RW_PASSES_INV)

    sa = _bmm_nt(lhs, s0, RW_PASSES_A)
    rhs_u = sa[:, :C] + _bmm(a_ak, v3, RW_PASSES_A)
    u = _bmm(inv, rhs_u, RW_PASSES_INV)
    y3 = sa[:, C:] + _bmm(r_b, u, RW_PASSES_A) + _bmm(r_k, v3, RW_PASSES_A)

    for h in range(RW_HEADS):
        sl = slice(h * 64, (h + 1) * 64)
        pe = p_end[:, sl]
        uv_t = jnp.concatenate([u[h], v3[h]], axis=1).T
        bk = jnp.concatenate([bt[h] * pe, kt[h] * pe], axis=1)
        upd = _mm(uv_t, bk, RW_PASSES_A)
        s_ref[h] = s0[h] * pe + upd[:64, :64] + upd[64:, 64:]
        y_ref[:, sl] = _group_norm_bonus(y3[h], r3[h], km3[h], v3[h], rk_ref[:, sl], gg_ref[:, sl], gb_ref[:, sl])

    @pl.when(c == nc - 1)
    def _():
        wkv_ref[0] = s_ref[...]


def _rwkv_prompt(proj, shift0, wkv0, rw, nb, t):
    chunk = 128
    assert t % chunk == 0
    nc = t // chunk
    vec = lambda n: pl.BlockSpec((1, n), lambda b, c: (0, 0))
    mat = lambda a, b_: pl.BlockSpec((a, b_), lambda b, c: (0, 0))
    return pl.pallas_call(
        functools.partial(_rwkv_chunk_kernel, chunk=chunk),
        grid=(nb, nc),
        in_specs=[pl.BlockSpec((chunk, RW_SHIFT_W), lambda b, c: (b * nc + c, 0)),
                  pl.BlockSpec((1, 1, RW_SHIFT_W), lambda b, c: (b, 0, 0)),
                  pl.BlockSpec((1, RW_HEADS, 64, 64), lambda b, c: (b, 0, 0, 0)),
                  vec(RW_SHIFT_W), vec(512), mat(64, 512), vec(512), mat(64, 512),
                  vec(512), vec(512), vec(512), vec(512), vec(512)],
        out_specs=[pl.BlockSpec((chunk, 512), lambda b, c: (b * nc + c, 0)),
                   pl.BlockSpec((1, RW_HEADS, 64, 64), lambda b, c: (b, 0, 0, 0))],
        out_shape=[jax.ShapeDtypeStruct((nb * t, 512), F32),
                   jax.ShapeDtypeStruct((nb, RW_HEADS, 64, 64), F32)],
        scratch_shapes=[pltpu.VMEM((1, RW_SHIFT_W), F32), pltpu.VMEM((RW_HEADS, 64, 64), F32)],
        compiler_params=_cparams(("parallel", "arbitrary")),
        name="rwkv_prompt",
    )(proj, shift0.reshape(nb, 1, RW_SHIFT_W), wkv0, *rw)


def _rwkv_step_kernel(p_ref, shift0_ref, wkv0_ref, mu_ref, w0_ref, w2_ref, a0_ref, a2_ref,
                      kk_ref, ka_ref, rk_ref, gg_ref, gb_ref,
                      y_ref, wkv_ref, r_s, k_s, v_s, wt_s, kt_s, kkt_s, bt_s, rt_s, vt_s, yt_s):
    h = pl.program_id(0)
    nh = pl.num_programs(0)

    @pl.when(h == 0)
    def _():
        r, k_mod, v, lw, a, kk = _rwkv_prep(p_ref[:, :RW_SHIFT_W], shift0_ref[...], mu_ref[...], w0_ref[...],
                                            w2_ref[...], a0_ref[...], a2_ref[...], kk_ref[...], ka_ref[...])
        kkn = jnp.concatenate([_head_norm(kk[:, g * 64:(g + 1) * 64]) for g in range(RW_HEADS)], axis=1)
        r_s[...] = r
        k_s[...] = k_mod
        v_s[...] = v
        wt_s[...] = jnp.exp(lw).T
        kt_s[...] = k_mod.T
        kkt_s[...] = kkn.T
        bt_s[...] = (kkn * a).T
        rt_s[...] = r.T
        vt_s[...] = v.T

    base = pl.multiple_of(h * 64, 64)
    w_t = wt_s[pl.ds(base, 64), :]
    k_t = kt_s[pl.ds(base, 64), :]
    kk_t = kkt_s[pl.ds(base, 64), :]
    b_t = bt_s[pl.ds(base, 64), :]
    r_t = rt_s[pl.ds(base, 64), :]

    def body(vi, carry):
        s0 = wkv0_ref[0, vi]
        sa = -jnp.sum(s0 * kk_t, axis=0, keepdims=True)
        s1 = s0 * w_t + sa * b_t + vt_s[pl.ds(base + vi, 1), :] * k_t
        wkv_ref[0, vi] = s1
        yt_s[pl.ds(base + vi, 1), :] = jnp.sum(s1 * r_t, axis=0, keepdims=True)
        return carry

    lax.fori_loop(0, RW_HEAD_DIM, body, 0)

    @pl.when(h == nh - 1)
    def _():
        y = yt_s[...].T
        for g in range(RW_HEADS):
            sl = slice(g * 64, (g + 1) * 64)
            y_ref[:, sl] = _group_norm_bonus(y[:, sl], r_s[:, sl], k_s[:, sl], v_s[:, sl],
                                             rk_ref[:, sl], gg_ref[:, sl], gb_ref[:, sl])


def _rwkv_sample(proj, shift0, wkv0_t, rw):
    nb = proj.shape[0]
    vec = lambda n: pl.BlockSpec((1, n), lambda i: (0, 0))
    mat = lambda a, b_: pl.BlockSpec((a, b_), lambda i: (0, 0))
    sq = pltpu.VMEM((nb, 512), F32)
    st = pltpu.VMEM((512, nb), F32)
    return pl.pallas_call(
        _rwkv_step_kernel,
        grid=(RW_HEADS,),
        in_specs=[pl.BlockSpec((nb, 1792), lambda i: (0, 0)),
                  mat(nb, RW_SHIFT_W),
                  pl.BlockSpec((1, 64, 64, nb), lambda i: (i, 0, 0, 0)),
                  vec(RW_SHIFT_W), vec(512), mat(64, 512), vec(512), mat(64, 512),
                  vec(512), vec(512), vec(512), vec(512), vec(512)],
        out_specs=[pl.BlockSpec((nb, 512), lambda i: (0, 0)),
                   pl.BlockSpec((1, 64, 64, nb), lambda i: (i, 0, 0, 0))],
        out_shape=[jax.ShapeDtypeStruct((nb, 512), F32),
                   jax.ShapeDtypeStruct((RW_HEADS, 64, 64, nb), F32)],
        scratch_shapes=[sq, sq, sq, st, st, st, st, st, st, st],
        compiler_params=_cparams(("arbitrary",)),
        name="rwkv_sample",
    )(proj, shift0, wkv0_t, *rw)


def _rms(x, g):
    return x * lax.rsqrt(jnp.mean(x * x, axis=-1, keepdims=True) + 1e-6) * g


def _mla_prep_kernel(p_ref, cos_ref, sin_ref, qn_ref, wuq_ref, kvn_ref, wukt_ref,
                     rows_ref, rowsb_ref, ct_ref, qf_ref):
    blk = p_ref[...]
    tm = blk.shape[0]
    qd = blk[:, 0:256]
    kvd = blk[:, 256:384]
    kr = blk[:, 384:416]
    q = _dot(_rms(qd, qn_ref[...]).astype(BF16), wuq_ref[...])
    cos = cos_ref[...]
    sin = sin_ref[...]
    x1 = q[:, 512:640]
    x2 = q[:, 640:768]
    r1 = x1 * cos - x2 * sin
    r2 = x2 * cos + x1 * sin
    c_kv = _rms(kvd, kvn_ref[...])
    c16 = cos[:, :16]
    s16 = sin[:, :16]
    k1 = kr[:, :16]
    k2 = kr[:, 16:32]
    k_rope = jnp.concatenate([k1 * c16 - k2 * s16, k2 * c16 + k1 * s16], axis=-1)
    rows_ref[...] = jnp.concatenate([c_kv, k_rope], axis=-1)
    pad = jnp.zeros((tm, 96), F32)
    rowsb_ref[...] = jnp.concatenate([c_kv, k_rope, pad], axis=-1).astype(BF16)
    ct_ref[...] = c_kv.T.astype(BF16)
    scale = (MLA_NOPE + MLA_ROPE) ** -0.5
    for h in range(MLA_HEADS):
        ql = _dot(q[:, h * 64:(h + 1) * 64].astype(BF16), wukt_ref[h])
        qr = jnp.concatenate([r1[:, h * 16:(h + 1) * 16], r2[:, h * 16:(h + 1) * 16]], axis=-1)
        qf_ref[h] = (jnp.concatenate([ql, qr, pad], axis=-1) * scale).astype(BF16)


def _mla_prep(proj, cos, sin, mw):
    m = proj.shape[0]
    tm = _pick(m, (512, 256, 128))
    q_norm, wuq, kv_norm, wukt = mw
    return pl.pallas_call(
        _mla_prep_kernel,
        grid=(m // tm,),
        in_specs=[pl.BlockSpec((tm, 512), lambda i: (i, OFF_MLA // 512)),
                  pl.BlockSpec((tm, 128), lambda i: (i, 0)),
                  pl.BlockSpec((tm, 128), lambda i: (i, 0)),
                  pl.BlockSpec((1, 256), lambda i: (0, 0)),
                  pl.BlockSpec((256, 768), lambda i: (0, 0)),
                  pl.BlockSpec((1, 128), lambda i: (0, 0)),
                  pl.BlockSpec((MLA_HEADS, 64, 128), lambda i: (0, 0, 0))],
        out_specs=[pl.BlockSpec((tm, MLA_ROW), lambda i: (i, 0)),
                   pl.BlockSpec((tm, 256), lambda i: (i, 0)),
                   pl.BlockSpec((MLA_KV_RANK, tm), lambda i: (0, i)),
                   pl.BlockSpec((MLA_HEADS, tm, 256), lambda i: (0, i, 0))],
        out_shape=[jax.ShapeDtypeStruct((m, MLA_ROW), F32),
                   jax.ShapeDtypeStruct((m, 256), BF16),
                   jax.ShapeDtypeStruct((MLA_KV_RANK, m), BF16),
                   jax.ShapeDtypeStruct((MLA_HEADS, m, 256), BF16)],
        compiler_params=_cparams(("parallel",)),
        name="mla_prep",
    )(proj, cos, sin, q_norm, wuq, kv_norm, wukt)


def _mla_attn_kernel(it_ref, jt_ref, q_ref, k_ref, ct_ref, wuvt_ref, y_ref, m_ref, l_ref, acc_ref, *, tq, tk):
    t = pl.program_id(1)
    i = it_ref[t]
    j = jt_ref[t]
    j_last = (i * tq) // tk
    nh = MLA_HEADS

    @pl.when(j == 0)
    def _():
        m_ref[...] = jnp.full_like(m_ref, NEG_BIG)
        l_ref[...] = jnp.zeros_like(l_ref)
        acc_ref[...] = jnp.zeros_like(acc_ref)

    def step(causal):
        q = q_ref[...].reshape(nh * tq, 256)
        s = _dot_nt(k_ref[...], q)
        if causal:
            key = lax.broadcasted_iota(jnp.int32, s.shape, 0)
            qry = lax.broadcasted_iota(jnp.int32, s.shape, 1) % tq
            s = jnp.where(key <= qry + (i * tq - j * tk), s, NEG_BIG)
        m_old = m_ref[...]
        m_new = jnp.maximum(m_old, jnp.max(s, axis=0, keepdims=True))
        alpha = jnp.exp(m_old - m_new)
        p = jnp.exp(s - m_new)
        l_ref[...] = alpha * l_ref[...] + jnp.sum(p, axis=0, keepdims=True)
        acc_ref[...] = alpha * acc_ref[...] + _dot(ct_ref[...], p.astype(BF16))
        m_ref[...] = m_new

    @pl.when(j < j_last)
    def _():
        step(False)

    @pl.when(j == j_last)
    def _():
        step(True)
        o_t = (acc_ref[...] / l_ref[...]).astype(BF16)
        y_t = jnp.concatenate([_dot(wuvt_ref[h], o_t[:, h * tq:(h + 1) * tq]) for h in range(nh)], axis=0)
        y_ref[...] = y_t.T


def _mla_attn_prompt(qf, rowsb, ct, wuvt, nb, t):
    tq = _pick(t, (256, 128))
    nq = t // tq
    tk = 2 * tq if t % (2 * tq) == 0 else tq
    nk = t // tk
    pairs = [(i, j) for i in range(nq) for j in range((i * tq) // tk + 1)]
    i_tab = jnp.asarray([p[0] for p in pairs], jnp.int32)
    j_tab = jnp.asarray([p[1] for p in pairs], jnp.int32)
    grid_spec = pltpu.PrefetchScalarGridSpec(
        num_scalar_prefetch=2,
        grid=(nb, len(pairs)),
        in_specs=[pl.BlockSpec((MLA_HEADS, tq, 256), lambda b, t_, it, jt: (0, b * nq + it[t_], 0)),
                  pl.BlockSpec((tk, 256), lambda b, t_, it, jt: (b * nk + jt[t_], 0)),
                  pl.BlockSpec((MLA_KV_RANK, tk), lambda b, t_, it, jt: (0, b * nk + jt[t_])),
                  pl.BlockSpec((MLA_HEADS, 64, 128), lambda b, t_, it, jt: (0, 0, 0))],
        out_specs=pl.BlockSpec((tq, 512), lambda b, t_, it, jt: (b * nq + it[t_], 0)),
        scratch_shapes=[pltpu.VMEM((1, MLA_HEADS * tq), F32), pltpu.VMEM((1, MLA_HEADS * tq), F32),
                        pltpu.VMEM((MLA_KV_RANK, MLA_HEADS * tq), F32)])
    return pl.pallas_call(
        functools.partial(_mla_attn_kernel, tq=tq, tk=tk),
        grid_spec=grid_spec,
        out_shape=jax.ShapeDtypeStruct((nb * t, 512), F32),
        compiler_params=_cparams(("parallel", "arbitrary")),
        name="mla_attn_prompt",
    )(i_tab, j_tab, qf, rowsb, ct, wuvt)


def _page_copy(cache_ref, buf_ref, sem_ref, row, slot, j):
    return pltpu.make_async_copy(cache_ref.at[row], buf_ref.at[slot, j], sem_ref.at[slot])


def _stream_step(rows_ref, cache_ref, buf_ref, sem_ref, *, pc):
    b = pl.program_id(0)
    c = pl.program_id(1)
    nc = pl.num_programs(1)
    step = b * nc + c
    last = pl.num_programs(0) * nc - 1
    slot = step % 2
    nxt = jnp.minimum(step + 1, last)
    nb_, nc_ = nxt // nc, nxt % nc

    @pl.when(step == 0)
    def _():
        for j in range(pc):
            _page_copy(cache_ref, buf_ref, sem_ref, rows_ref[b, c * pc + j], slot, j).start()

    for j in range(pc):
        _page_copy(cache_ref, buf_ref, sem_ref, 0, slot, j).wait()

    def prefetch(j):
        _page_copy(cache_ref, buf_ref, sem_ref, rows_ref[nb_, nc_ * pc + j], 1 - slot, j).start()

    def drain():
        @pl.when(step == last)
        def _():
            for j in range(pc):
                _page_copy(cache_ref, buf_ref, sem_ref, 0, 1 - slot, j).wait()

    return slot, prefetch, drain


def _mla_decode_kernel(pt_ref, q_ref, new_ref, cache_ref, o_ref, buf_ref, sem_ref, m_ref, l_ref, acc_ref,
                       *, pc):
    c = pl.program_id(1)
    nc = pl.num_programs(1)
    slot, prefetch, drain = _stream_step(pt_ref, cache_ref, buf_ref, sem_ref, pc=pc)

    @pl.when(c == 0)
    def _():
        m_ref[...] = jnp.full_like(m_ref, NEG_BIG)
        l_ref[...] = jnp.zeros_like(l_ref)
        acc_ref[...] = jnp.zeros_like(acc_ref)

    q = q_ref[0]
    grp = _pick(pc, (16, 8, 4, 2, 1))
    qf = jnp.broadcast_to(q[None, :, :MLA_ROW], (grp, MLA_HEADS, MLA_ROW))
    kts, ss = [], []
    for g0 in range(0, pc, grp):
        for j in range(g0, g0 + grp):
            prefetch(j)
        kt_g = buf_ref[slot, g0:g0 + grp].astype(BF16)
        kts.append(kt_g)
        ss.append(jnp.einsum('phk,pkn->phn', qf, kt_g, preferred_element_type=F32))
    kt = jnp.concatenate(kts, axis=0)
    kc = kt[:, :MLA_KV_RANK, :]
    s = jnp.concatenate(ss, axis=0)
    m_old = m_ref[...]
    m_new = jnp.maximum(m_old, jnp.max(jnp.max(s, axis=0), axis=-1, keepdims=True))
    alpha = jnp.exp(m_old - m_new)
    p = jnp.exp(s - m_new[None])
    l_new = alpha * l_ref[...] + jnp.sum(jnp.sum(p, axis=0), axis=-1, keepdims=True)
    pv = jnp.einsum('phn,pcn->phc', p.astype(BF16), kc, preferred_element_type=F32)
    acc_new = alpha * acc_ref[...] + jnp.sum(pv, axis=0)
    m_ref[...] = m_new
    l_ref[...] = l_new
    acc_ref[...] = acc_new

    @pl.when(c == nc - 1)
    def _():
        new = new_ref[0].astype(F32)
        s_n = jnp.sum(q.astype(F32) * new, axis=-1, keepdims=True)
        m_f = jnp.maximum(m_new, s_n)
        al = jnp.exp(m_new - m_f)
        p_n = jnp.exp(s_n - m_f)
        l_f = al * l_new + p_n
        acc_f = al * acc_new + p_n.astype(BF16).astype(F32) * new[:, :MLA_KV_RANK]
        o_ref[0] = acc_f / l_f

    drain()


def _mla_decode(page_table, qf_s, rowsb_s, cache):
    nb, n_pages = page_table.shape
    pc = _pick(n_pages, (128, 64, 32, 16, 8, 4, 2, 1))
    nc = n_pages // pc
    grid_spec = pltpu.PrefetchScalarGridSpec(
        num_scalar_prefetch=1,
        grid=(nb, nc),
        in_specs=[pl.BlockSpec((1, MLA_HEADS, 256), lambda b, c, pt: (b, 0, 0)),
                  pl.BlockSpec((1, 1, 256), lambda b, c, pt: (b, 0, 0)),
                  pl.BlockSpec(memory_space=pl.ANY)],
        out_specs=pl.BlockSpec((1, MLA_HEADS, MLA_KV_RANK), lambda b, c, pt: (b, 0, 0)),
        scratch_shapes=[pltpu.VMEM((2, pc, MLA_ROW, PAGE_SIZE), F32),
                        pltpu.SemaphoreType.DMA((2,)),
                        pltpu.VMEM((MLA_HEADS, 1), F32), pltpu.VMEM((MLA_HEADS, 1), F32),
                        pltpu.VMEM((MLA_HEADS, MLA_KV_RANK), F32)])
    return pl.pallas_call(
        functools.partial(_mla_decode_kernel, pc=pc),
        grid_spec=grid_spec,
        out_shape=jax.ShapeDtypeStruct((nb, MLA_HEADS, MLA_KV_RANK), F32),
        compiler_params=_cparams(("arbitrary", "arbitrary")),
        name="mla_decode",
    )(page_table, qf_s, rowsb_s.reshape(nb, 1, 256), cache)


def _mla_out_kernel(o_ref, wuv_ref, y_ref):
    for h in range(MLA_HEADS):
        y_ref[:, h * 64:(h + 1) * 64] = _dot(o_ref[h].astype(BF16), wuv_ref[h])


def _mla_out(o_hm, wuv):
    m = o_hm.shape[1]
    return pl.pallas_call(
        _mla_out_kernel,
        grid=(1,),
        in_specs=[pl.BlockSpec((MLA_HEADS, m, 128), lambda i: (0, 0, 0)),
                  pl.BlockSpec((MLA_HEADS, 128, 64), lambda i: (0, 0, 0))],
        out_specs=pl.BlockSpec((m, 512), lambda i: (0, 0)),
        out_shape=jax.ShapeDtypeStruct((m, 512), F32),
        compiler_params=_cparams(("arbitrary",)),
        name="mla_out",
    )(o_hm, wuv)


def _topk_select(gate, gate_row, n_valid, topk):
    nblk = gate.shape[0]
    blk = lax.broadcasted_iota(jnp.int32, gate.shape, 0)
    rank = jnp.zeros(gate.shape, F32)
    for m in range(nblk):
        g_m = gate_row(m)
        ahead = jnp.where(g_m > gate, 1.0, jnp.where((g_m == gate) & (m < blk), 1.0, 0.0))
        rank = rank + ahead * jnp.where(m < n_valid, 1.0, 0.0)
    return jnp.where((rank < topk) & (blk < n_valid), 1.0, 0.0)


def _moba_prompt_kernel(q_ref, kv_ref, slope_ref, y_ref, vt_ref, kb_ref, km_ref, gate_ref, sel_ref, *, nblk):
    i = pl.program_id(1)
    nq = MB_GROUP * MB_BLOCK
    scale = MB_HEAD_DIM ** -0.5

    @pl.when(i == 0)
    def _():
        for n in range(nblk):
            blk = kv_ref[n * MB_BLOCK:(n + 1) * MB_BLOCK, :]
            vt_ref[n] = blk[:, 128:256].T.astype(BF16)
            kb_ref[n * MB_BLOCK:(n + 1) * MB_BLOCK, :] = blk[:, 0:128].astype(BF16)
            km_ref[n:n + 1, :] = jnp.mean(blk[:, 0:128], axis=0, keepdims=True)

    q = q_ref[...]
    qoff = (lax.broadcasted_iota(jnp.int32, (1, nq), 1) % MB_BLOCK).astype(F32)
    koff = lax.broadcasted_iota(jnp.int32, (MB_BLOCK, 1), 0).astype(F32)
    outs = []
    qsb, slope, bias0 = [], [], []
    for g in range(MB_KV_HEADS):
        qs = jnp.concatenate([q[:, (g * MB_GROUP + e) * 64:(g * MB_GROUP + e + 1) * 64]
                              for e in range(MB_GROUP)], axis=0)
        qsb.append((qs * scale).astype(BF16))
        slope.append(slope_ref[g:g + 1, :])
        bias0.append(slope[g] * koff)
        gate = _dot_nt(km_ref[:, g * 64:(g + 1) * 64], qs, precision=HIGHEST)
        gate_ref[...] = gate
        sel_ref[g] = _topk_select(gate, lambda m: gate_ref[m:m + 1, :], i, MB_TOPK)

    def block_scores(g, n):
        kb = kb_ref[pl.ds(pl.multiple_of(n * MB_BLOCK, MB_BLOCK), MB_BLOCK), g * 64:(g + 1) * 64]
        return _dot_nt(kb, qsb[g]) + bias0[g]

    def accumulate(g, blocks, carry):
        m_old, l_old, acc = carry
        m_new = m_old
        for _, s, rb in blocks:
            m_new = jnp.maximum(m_new, jnp.max(s, axis=0, keepdims=True) + rb)
        alpha = jnp.exp(m_old - m_new)
        ps = [jnp.exp(s - (m_new - rb)) for _, s, rb in blocks]
        l_new = alpha * l_old
        for p in ps:
            l_new = l_new + jnp.sum(p, axis=0, keepdims=True)
        vt = jnp.concatenate([vt_ref[n][g * 64:(g + 1) * 64, :] for n, _, _ in blocks], axis=1)
        pcat = jnp.concatenate([p.astype(BF16) for p in ps], axis=0)
        return m_new, l_new, alpha * acc + _dot(vt, pcat)

    def row_bias(g, n):
        return slope[g] * (n * MB_BLOCK).astype(F32) + jnp.where(sel_ref[g, pl.ds(n, 1), :] > 0.5, 0.0, NEG_BIG)

    carry = []
    for g in range(MB_KV_HEADS):
        s_own = jnp.where(koff <= qoff, block_scores(g, i), NEG_BIG)
        init = (jnp.full((1, nq), NEG_BIG, F32), jnp.zeros((1, nq), F32), jnp.zeros((64, nq), F32))
        carry.append(accumulate(g, [(i, s_own, slope[g] * (i * MB_BLOCK).astype(F32))], init))

    def body(n2, carry):
        n0 = 2 * n2
        n1 = n0 + 1
        return tuple(accumulate(g, [(n0, block_scores(g, n0), row_bias(g, n0)),
                                    (n1, block_scores(g, n1), row_bias(g, n1))], carry[g])
                     for g in range(MB_KV_HEADS))

    carry = lax.fori_loop(0, (i + 1) // 2, body, tuple(carry))
    outs = [acc / l_f for _, l_f, acc in carry]

    ot = jnp.concatenate(outs, axis=0).T
    for g in range(MB_KV_HEADS):
        for e in range(MB_GROUP):
            hh = g * MB_GROUP + e
            y_ref[:, hh * 64:(hh + 1) * 64] = ot[e * MB_BLOCK:(e + 1) * MB_BLOCK, g * 64:(g + 1) * 64]


def _moba_prompt(proj, slopes_row, nb, t):
    assert t % MB_BLOCK == 0
    nblk = t // MB_BLOCK
    return pl.pallas_call(
        functools.partial(_moba_prompt_kernel, nblk=nblk),
        grid=(nb, nblk),
        in_specs=[pl.BlockSpec((MB_BLOCK, 512), lambda b, i: (b * nblk + i, OFF_MQ // 512)),
                  pl.BlockSpec((t, 256), lambda b, i: (b, OFF_KV // 256)),
                  pl.BlockSpec((MB_KV_HEADS, MB_GROUP * MB_BLOCK), lambda b, i: (0, 0))],
        out_specs=pl.BlockSpec((MB_BLOCK, 512), lambda b, i: (b * nblk + i, 0)),
        out_shape=jax.ShapeDtypeStruct((nb * t, 512), F32),
        scratch_shapes=[pltpu.VMEM((nblk, 128, MB_BLOCK), BF16), pltpu.VMEM((t, 128), BF16),
                        pltpu.VMEM((nblk, 128), F32),
                        pltpu.VMEM((nblk, MB_GROUP * MB_BLOCK), F32),
                        pltpu.VMEM((MB_KV_HEADS, nblk, MB_GROUP * MB_BLOCK), F32)],
        compiler_params=_cparams(("parallel", "arbitrary")),
        name="moba_prompt",
    )(proj, proj, slopes_row)


def _moba_keys_kernel(pt_ref, q_ref, knew_ref, slope_ref, cache_ref, p_ref, pown_ref, idx_ref,
                      buf_ref, sem_ref, s_ref, gate_ref, *, pc, past_len):
    slot, prefetch, drain = _stream_step(pt_ref, cache_ref, buf_ref, sem_ref, pc=pc)
    for j in range(pc):
        prefetch(j)
    ppb = MB_BLOCK // PAGE_SIZE
    nblk = pc // ppb
    scale = MB_HEAD_DIM ** -0.5
    q = q_ref[0]
    qb = (q * scale).astype(BF16)
    q_hi, q_lo = _split_bf16(q)
    lhs = jnp.concatenate([qb, q_hi, q_lo], axis=0)
    for n in range(nblk):
        g_hi = None
        lo_sum = None
        for e in range(ppb):
            kt = buf_ref[slot, n * ppb + e]
            kt_hi = kt.astype(BF16)
            r = _dot(lhs, kt_hi)
            s_ref[n * ppb + e] = r[0:8]
            g = r[8:16] + r[16:24]
            lo = kt - kt_hi.astype(F32)
            g_hi = g if g_hi is None else g_hi + g
            lo_sum = lo if lo_sum is None else lo_sum + lo
        g3 = g_hi + _dot(q_hi, lo_sum.astype(BF16))
        gate_ref[n] = jnp.sum(g3, axis=-1, keepdims=True)
    s = s_ref[...]
    gate = gate_ref[...]

    blk_f = lax.broadcasted_iota(jnp.int32, gate.shape, 0).astype(F32)
    sel = jnp.zeros(gate.shape, F32)
    picks = []
    for _ in range(MB_TOPK):
        top = jnp.max(gate, axis=0, keepdims=True)
        pick = jnp.min(jnp.where(gate == top, blk_f, float(nblk)), axis=0, keepdims=True)
        hit = blk_f == pick
        sel = jnp.where(hit, 1.0, sel)
        gate = jnp.where(hit, NEG_BIG, gate)
        picks.append(pick)
    idx_ref[0] = jnp.concatenate(picks, axis=0).astype(jnp.int32)

    shp = (nblk, ppb, MB_HEADS, PAGE_SIZE)
    pos = (lax.broadcasted_iota(jnp.int32, shp, 0) * MB_BLOCK + lax.broadcasted_iota(jnp.int32, shp, 1) * PAGE_SIZE
           + lax.broadcasted_iota(jnp.int32, shp, 3))
    dist = (past_len - pos).astype(F32)
    slope = slope_ref[...]
    s_m = jnp.where(sel[:, None] > 0.5, s.reshape(shp) - slope[None, None] * dist, NEG_BIG)
    s_own = jnp.sum(qb.astype(F32) * knew_ref[0].astype(BF16).astype(F32), axis=-1, keepdims=True)
    m = jnp.maximum(jnp.max(jnp.max(s_m, axis=(0, 1)), axis=-1, keepdims=True), s_own)
    e = jnp.exp(s_m - m[None, None])
    e_own = jnp.exp(s_own - m)
    inv = 1.0 / (jnp.sum(jnp.sum(e, axis=(0, 1)), axis=-1, keepdims=True) + e_own)
    p_ref[0] = (e * inv[None, None]).reshape(pc, MB_HEADS, PAGE_SIZE)
    pown_ref[0] = jnp.broadcast_to(e_own * inv, (MB_HEADS, 128))
    drain()


def _moba_keys(page_table, q_pad, k_new, slopes_col, cache):
    nb, n_pages = page_table.shape
    pc = n_pages
    past_len = n_pages * PAGE_SIZE
    assert past_len // MB_BLOCK >= MB_TOPK
    grid_spec = pltpu.PrefetchScalarGridSpec(
        num_scalar_prefetch=1,
        grid=(nb, 1),
        in_specs=[pl.BlockSpec((1, MB_HEADS, 128), lambda b, c, pt: (b, 0, 0)),
                  pl.BlockSpec((1, 1, 128), lambda b, c, pt: (b, 0, 0)),
                  pl.BlockSpec((MB_HEADS, 1), lambda b, c, pt: (0, 0)),
                  pl.BlockSpec(memory_space=pl.ANY)],
        out_specs=[pl.BlockSpec((1, pc, MB_HEADS, PAGE_SIZE), lambda b, c, pt: (b, 0, 0, 0)),
                   pl.BlockSpec((1, MB_HEADS, 128), lambda b, c, pt: (b, 0, 0)),
                   pl.BlockSpec((1, MB_TOPK, MB_HEADS, 1), lambda b, c, pt: (b, 0, 0, 0))],
        scratch_shapes=[pltpu.VMEM((2, pc, 128, PAGE_SIZE), F32),
                        pltpu.SemaphoreType.DMA((2,)),
                        pltpu.VMEM((pc, MB_HEADS, PAGE_SIZE), F32),
                        pltpu.VMEM((pc * PAGE_SIZE // MB_BLOCK, MB_HEADS, 1), F32)])
    return pl.pallas_call(
        functools.partial(_moba_keys_kernel, pc=pc, past_len=past_len),
        grid_spec=grid_spec,
        out_shape=[jax.ShapeDtypeStruct((nb, pc, MB_HEADS, PAGE_SIZE), F32),
                   jax.ShapeDtypeStruct((nb, MB_HEADS, 128), F32),
                   jax.ShapeDtypeStruct((nb, MB_TOPK, MB_HEADS, 1), jnp.int32)],
        compiler_params=_cparams(("arbitrary", "arbitrary")),
        name="moba_keys",
    )(page_table, q_pad, k_new.reshape(nb, 1, 128), slopes_col, cache)


def _moba_values_kernel(pt_ref, idx_ref, p_ref, pown_ref, vnew_ref, cache_ref, o_ref, buf_ref, sem_ref):
    b = pl.program_id(0)
    nb = pl.num_programs(0)
    slot = b % 2
    ppb = MB_BLOCK // PAGE_SIZE
    n_half = MB_HEADS * MB_TOPK * ppb

    def half_page(row, h, sl, i):
        return pltpu.make_async_copy(cache_ref.at[row, pl.ds((h // MB_GROUP) * 64, 64), :],
                                     buf_ref.at[sl, i], sem_ref.at[sl])

    def fetch(bb, sl):
        for h in range(MB_HEADS):
            for j in range(MB_TOPK):
                blk = idx_ref[bb, h * MB_TOPK + j]
                for e in range(ppb):
                    half_page(pt_ref[bb, blk * ppb + e], h, sl, (h * MB_TOPK + j) * ppb + e).start()

    @pl.when(b == 0)
    def _():
        fetch(b, slot)

    @pl.when(b + 1 < nb)
    def _():
        fetch(b + 1, 1 - slot)

    for i in range(n_half):
        half_page(0, 0, slot, i).wait()

    row_id = lax.broadcasted_iota(jnp.int32, (MB_HEADS, 1), 0)
    acc = jnp.zeros((MB_HEADS, 64), F32)
    for h in range(MB_HEADS):
        vts, prs = [], []
        for j in range(MB_TOPK):
            blk = idx_ref[b, h * MB_TOPK + j]
            for e in range(ppb):
                vts.append(buf_ref[slot, (h * MB_TOPK + j) * ppb + e])
                prs.append(p_ref[0, blk * ppb + e, h:h + 1, :])
        vt = jnp.concatenate(vts, axis=1).astype(BF16)
        pr = jnp.concatenate(prs, axis=1)
        lhs = jnp.where(row_id == h, pr, 0.0).astype(BF16)
        acc = acc + _dot_nt(lhs, vt)
    v_new = vnew_ref[0].astype(BF16).astype(F32)
    v8 = jnp.where(row_id < MB_GROUP, v_new[:, 0:64], v_new[:, 64:128])
    o_ref[0] = acc + pown_ref[0][:, 0:64].astype(BF16).astype(F32) * v8


def _moba_values(page_table, idx, p, p_own, v_new, cache):
    nb, n_pages = page_table.shape
    ppb = MB_BLOCK // PAGE_SIZE
    n_half = MB_HEADS * MB_TOPK * ppb
    grid_spec = pltpu.PrefetchScalarGridSpec(
        num_scalar_prefetch=2,
        grid=(nb,),
        in_specs=[pl.BlockSpec((1, n_pages, MB_HEADS, PAGE_SIZE), lambda b, pt, ix: (b, 0, 0, 0)),
                  pl.BlockSpec((1, MB_HEADS, 128), lambda b, pt, ix: (b, 0, 0)),
                  pl.BlockSpec((1, 1, 128), lambda b, pt, ix: (b, 0, 0)),
                  pl.BlockSpec(memory_space=pl.ANY)],
        out_specs=pl.BlockSpec((1, MB_HEADS, 64), lambda b, pt, ix: (b, 0, 0)),
        scratch_shapes=[pltpu.VMEM((2, n_half, 64, PAGE_SIZE), F32),
                        pltpu.SemaphoreType.DMA((2,))])
    return pl.pallas_call(
        _moba_values_kernel,
        grid_spec=grid_spec,
        out_shape=jax.ShapeDtypeStruct((nb, MB_HEADS, 64), F32),
        compiler_params=_cparams(("arbitrary",)),
        name="moba_values",
    )(page_table, idx, p, p_own, v_new.reshape(nb, 1, 128), cache)


def _merge_kernel(ya_ref, yb_ref, yc_ref, gate_ref, mg_ref, x_ref, wb_ref, wo_ref, g_ref, b_ref, o_ref, *, alpha):
    merged = None
    for n, y_ref in enumerate((ya_ref, yb_ref, yc_ref)):
        gcol = gate_ref[:, n * BR_WIDTH:(n + 1) * BR_WIDTH]
        o = y_ref[...] * (gcol * _sigmoid(gcol))
        br = _dot(o.astype(BF16), wb_ref[n])
        term = br * _sigmoid(mg_ref[:, n * D_MODEL:(n + 1) * D_MODEL])
        merged = term if merged is None else merged + term
    z = alpha * x_ref[...] + _dot(merged.astype(BF16), wo_ref[...])
    mu = jnp.mean(z, axis=-1, keepdims=True)
    zc = z - mu
    var = jnp.mean(zc * zc, axis=-1, keepdims=True)
    o_ref[...] = zc * lax.rsqrt(var + 1e-5) * g_ref[...] + b_ref[...]


def _merge(ya, yb, yc, proj, x, wb, wo, ln_g, ln_b, alpha):
    m = x.shape[0]
    tm = _pick(m, (512, 256, 128))
    row = lambda w, j=0: pl.BlockSpec((tm, w), lambda i: (i, j))
    return pl.pallas_call(
        functools.partial(_merge_kernel, alpha=alpha),
        grid=(m // tm,),
        in_specs=[row(512), row(512), row(512),
                  row(1536, OFF_GATE // 1536), row(3072, OFF_MERGE // 3072), row(D_MODEL),
                  pl.BlockSpec((N_BRANCH, BR_WIDTH, D_MODEL), lambda i: (0, 0, 0)),
                  pl.BlockSpec((D_MODEL, D_MODEL), lambda i: (0, 0)),
                  pl.BlockSpec((1, D_MODEL), lambda i: (0, 0)),
                  pl.BlockSpec((1, D_MODEL), lambda i: (0, 0))],
        out_specs=row(D_MODEL),
        out_shape=jax.ShapeDtypeStruct((m, D_MODEL), F32),
        compiler_params=_cparams(("parallel",)),
        name="merge",
    )(ya, yb, yc, proj, proj, x, wb, wo, ln_g, ln_b)


def _rope_tables(pos):
    half = MLA_ROPE // 2
    inv = ROPE_THETA ** (-jnp.arange(half, dtype=F32) / half)
    ang = pos.astype(F32)[:, None] * inv[None, :]
    return jnp.tile(jnp.cos(ang), (1, MLA_HEADS)), jnp.tile(jnp.sin(ang), (1, MLA_HEADS))


def _permute_w_in(w):
    d = w.shape[0]
    z = lambda n: jnp.zeros((d, n), w.dtype)
    return jnp.concatenate([w[:, 0:1664], z(128), w[:, 2592:2848], w[:, 1664:2080], z(96),
                            w[:, 2080:2592], w[:, 4384:7456], w[:, 2848:4384]], axis=1).astype(BF16)


def kernel(x_prompt, x_sample, cache_mla, cache_moba_k, cache_moba_v, state_wkv, state_shift, page_table, w_in, rw_mu, rw_w0, rw_w2, rw_a0, rw_a2, rw_k_k, rw_k_a, rw_r_k, rw_gn_g, rw_gn_b, mla_q_norm, mla_w_uq, mla_kv_norm, mla_w_uk, mla_w_uv, w_branch, w_out, ln_g, ln_b):
    bp, tp, d = x_prompt.shape
    bs, ts, _ = x_sample.shape
    assert ts == 1 and d == D_MODEL
    depth = w_in.shape[0]
    n_pool = cache_mla.shape[0]
    n_pages = page_table.shape[1]
    past_len = n_pages * PAGE_SIZE
    assert past_len % MB_BLOCK == 0
    alpha = (2 * depth) ** 0.25

    cache_mla2 = jnp.transpose(cache_mla, (0, 1, 3, 2)).reshape(n_pool * depth, MLA_ROW, PAGE_SIZE)
    cache_k2 = jnp.transpose(cache_moba_k, (0, 1, 3, 4, 2)).reshape(n_pool * depth, 128, PAGE_SIZE)
    cache_v2 = jnp.transpose(cache_moba_v, (0, 1, 3, 4, 2)).reshape(n_pool * depth, 128, PAGE_SIZE)

    cos_p, sin_p = _rope_tables(jnp.tile(jnp.arange(tp), bp))
    cos_s, sin_s = _rope_tables(jnp.full((bs,), past_len))
    slopes = 2.0 ** (-8.0 * jnp.arange(1, MB_HEADS + 1, dtype=F32) / MB_HEADS)
    slopes_row = jnp.repeat(slopes.reshape(MB_KV_HEADS, MB_GROUP), MB_BLOCK, axis=1)
    slopes_col = slopes.reshape(MB_HEADS, 1)
    half = MLA_ROPE // 2
    hd = MLA_NOPE + MLA_ROPE

    xp = x_prompt.reshape(bp * tp, d)
    xs = x_sample.reshape(bs, d)
    zero_shift = jnp.zeros((bp, RW_SHIFT_W), F32)
    zero_wkv = jnp.zeros((bp, RW_HEADS, 64, 64), F32)

    mla_p, mla_s, kp, ksm, vp, vsm, wp, wsm, sp, ssm = [], [], [], [], [], [], [], [], [], []
    for l in range(depth):
        w_perm = _permute_w_in(w_in[l])
        r2 = lambda a: a.reshape(1, -1)
        rw = (r2(rw_mu[l]), r2(rw_w0[l]), rw_w2[l].astype(BF16), r2(rw_a0[l]), rw_a2[l].astype(BF16),
              r2(rw_k_k[l]), r2(rw_k_a[l]), r2(rw_r_k[l]), r2(rw_gn_g[l]), r2(rw_gn_b[l]))
        wuq3 = mla_w_uq[l].reshape(MLA_Q_RANK, MLA_HEADS, hd)
        wuq = jnp.concatenate([wuq3[:, :, :MLA_NOPE].reshape(MLA_Q_RANK, -1),
                               wuq3[:, :, MLA_NOPE:MLA_NOPE + half].reshape(MLA_Q_RANK, -1),
                               wuq3[:, :, MLA_NOPE + half:].reshape(MLA_Q_RANK, -1)], axis=1).astype(BF16)
        wukt = jnp.transpose(mla_w_uk[l], (1, 2, 0)).astype(BF16)
        wuv = jnp.transpose(mla_w_uv[l], (1, 0, 2)).astype(BF16)
        wuvt = jnp.transpose(mla_w_uv[l], (1, 2, 0)).astype(BF16)
        mw = (r2(mla_q_norm[l]), wuq, r2(mla_kv_norm[l]), wukt)
        wb = w_branch[l].astype(BF16)
        wo = w_out[l].astype(BF16)
        lg, lb = r2(ln_g[l]), r2(ln_b[l])

        proj = _inproj(xp, w_perm)
        ya, wkv_new = _rwkv_prompt(proj, zero_shift, zero_wkv, rw, bp, tp)
        rows, rowsb, ct, qf = _mla_prep(proj, cos_p, sin_p, mw)
        yb = _mla_attn_prompt(qf, rowsb, ct, wuvt, bp, tp)
        yc = _moba_prompt(proj, slopes_row, bp, tp)
        xp_new = _merge(ya, yb, yc, proj, xp, wb, wo, lg, lb, alpha)
        mla_p.append(rows.reshape(bp, tp, MLA_ROW))
        kp.append(proj[:, OFF_KV:OFF_KV + 128].reshape(bp, tp, MB_KV_HEADS, MB_HEAD_DIM))
        vp.append(proj[:, OFF_KV + 128:OFF_KV + 256].reshape(bp, tp, MB_KV_HEADS, MB_HEAD_DIM))
        wp.append(wkv_new)
        sp.append(proj.reshape(bp, tp, PROJ_W)[:, tp - 1, :RW_SHIFT_W])
        xp = xp_new

        proj = _inproj(xs, w_perm)
        ya, wkv_t = _rwkv_sample(proj, state_shift[l], jnp.transpose(state_wkv[l], (1, 2, 3, 0)), rw)
        wkv_new = jnp.transpose(wkv_t, (3, 0, 1, 2))
        rows, rowsb, _, qf = _mla_prep(proj, cos_s, sin_s, mw)
        page_rows = page_table * depth + l
        o_lat = _mla_decode(page_rows, jnp.transpose(qf, (1, 0, 2)), rowsb, cache_mla2)
        yb = _mla_out(jnp.transpose(o_lat, (1, 0, 2)), wuv)
        k_new = proj[:, OFF_KV:OFF_KV + 128]
        v_new = proj[:, OFF_KV + 128:OFF_KV + 256]
        q4 = proj[:, OFF_MQ:OFF_MQ + 512].reshape(bs, MB_KV_HEADS, MB_GROUP, MB_HEAD_DIM)
        q_pad = jnp.concatenate(
            [jnp.pad(q4[:, g], ((0, 0), (0, 0), (g * 64, (MB_KV_HEADS - 1 - g) * 64))) for g in range(MB_KV_HEADS)],
            axis=1)
        p_att, p_own, idx = _moba_keys(page_rows, q_pad, k_new, slopes_col, cache_k2)
        idx = jnp.transpose(idx[..., 0], (0, 2, 1)).reshape(bs, MB_HEADS * MB_TOPK)
        yc = _moba_values(page_rows, idx, p_att, p_own, v_new, cache_v2).reshape(bs, MB_HEADS * 64)
        xs_new = _merge(ya, yb, yc, proj, xs, wb, wo, lg, lb, alpha)
        mla_s.append(rows.reshape(bs, 1, MLA_ROW))
        ksm.append(k_new.reshape(bs, 1, MB_KV_HEADS, MB_HEAD_DIM))
        vsm.append(v_new.reshape(bs, 1, MB_KV_HEADS, MB_HEAD_DIM))
        wsm.append(wkv_new)
        ssm.append(proj[:, :RW_SHIFT_W])
        xs = xs_new

    return (xp.reshape(bp, tp, d), xs.reshape(bs, 1, d),
            jnp.stack(mla_p, 1), jnp.stack(mla_s, 1),
            jnp.stack(kp, 1), jnp.stack(ksm, 1), jnp.stack(vp, 1), jnp.stack(vsm, 1),
            jnp.stack(wp, 0), jnp.stack(wsm, 0), jnp.stack(sp, 0), jnp.stack(ssm, 0))
```
